```python
import math
import numpy as np
import jax
import jax.numpy as jnp
from jax import lax

D_MODEL = 1024
BATCH = 4
SEQ = 4096
DEPTH = 4
DEC_BATCH = 32
DEC_SEQ = 1
PAST_LEN = 8192
PAGE_SIZE = 128

HEAD_DIM = 64
FOX_HEADS = D_MODEL // HEAD_DIM
NSA_HEADS = D_MODEL // HEAD_DIM
NSA_GROUPS = 4
NSA_REP = NSA_HEADS // NSA_GROUPS
NSA_CMP_LEN = 32
NSA_CMP_STRIDE = 16
NSA_SEL_LEN = 64
NSA_TOP_N = 16
NSA_N_LOCAL = 2
NSA_WINDOW = 512
DIFF_HEADS = D_MODEL // (2 * HEAD_DIM)
D_FF = 4 * D_MODEL
ROPE_THETA = 10000.0
NORM_EPS = 1e-6
Q_BLOCK = 128
SEL_Q_BLOCK = 64
POOL_NUM = 5
POOL_DEN = 4
DIFF_LAYER = 2

kernel_name = "hybrid_fox_nsa_diffattn_step"

F32 = jnp.float32
SCALE = HEAD_DIM ** -0.5


def rmsnorm(x, g):
    xf = x.astype(F32)
    y = xf * lax.rsqrt(jnp.mean(xf * xf, axis=-1, keepdims=True) + NORM_EPS)
    return (y * g.astype(F32)).astype(x.dtype)


def rope(x, pos):
    half = x.shape[-1] // 2
    inv = ROPE_THETA ** (-jnp.arange(half, dtype=F32) / half)
    ang = pos.astype(F32)[:, None] * inv[None, :]
    cos = jnp.cos(ang)[:, None, :]
    sin = jnp.sin(ang)[:, None, :]
    xf = x.astype(F32)
    x1, x2 = xf[..., :half], xf[..., half:]
    return jnp.concatenate([x1 * cos - x2 * sin, x2 * cos + x1 * sin], axis=-1).astype(x.dtype)


def masked_softmax(logits, mask):
    logits = jnp.where(mask, logits.astype(F32), -jnp.inf)
    m = jnp.max(logits, axis=-1, keepdims=True)
    m = jnp.where(jnp.isfinite(m), m, 0.0)
    e = jnp.exp(logits - m)
    return e / jnp.maximum(jnp.sum(e, axis=-1, keepdims=True), 1e-30)


def gather_pages(pool, page_table):
    rows = pool[page_table]
    return rows.reshape((page_table.shape[0], page_table.shape[1] * pool.shape[1]) + pool.shape[2:])


def sqrelu_mlp(x, w_up, w_down):
    h = jax.nn.relu(x @ w_up)
    return (h * h) @ w_down


def fox_project(x, w_in, b_f):
    b, t, _ = x.shape
    z = x @ w_in
    hd = FOX_HEADS * HEAD_DIM
    q = z[..., :hd].reshape(b, t, FOX_HEADS, HEAD_DIM)
    k = z[..., hd:2 * hd].reshape(b, t, FOX_HEADS, HEAD_DIM)
    v = z[..., 2 * hd:3 * hd].reshape(b, t, FOX_HEADS, HEAD_DIM)
    logf = jax.nn.log_sigmoid((z[..., 3 * hd:] + b_f).astype(F32))
    return q, k, v, logf


def fox_attend(q, k, v, cq, ck, q_pos, k_pos):
    s = jnp.einsum("bqhd,bkhd->bhqk", q, k, preferred_element_type=F32) * SCALE
    s = s + jnp.swapaxes(cq, 1, 2)[..., :, None] - jnp.swapaxes(ck, 1, 2)[..., None, :]
    p = masked_softmax(s, k_pos[None, :] <= q_pos[:, None])
    return jnp.einsum("bhqk,bkhd->bqhd", p.astype(v.dtype), v)


def fox_mixer(xp, xs, cache_k, cache_v, cache_logf, page_table, w_in, b_f, w_o):
    bp, sp, _ = xp.shape
    bs, ds, _ = xs.shape
    past = page_table.shape[1] * cache_k.shape[1]
    q, k, v, logf = fox_project(xp, w_in, b_f)
    c = jnp.cumsum(logf, axis=1)
    pos = jnp.arange(sp, dtype=jnp.int32)
    nb = sp // Q_BLOCK
    ob = lax.map(lambda a: fox_attend(a[0], k, v, a[1], c, a[2], pos),
                 (q.reshape(bp, nb, Q_BLOCK, FOX_HEADS, HEAD_DIM).swapaxes(0, 1),
                  c.reshape(bp, nb, Q_BLOCK, FOX_HEADS).swapaxes(0, 1),
                  pos.reshape(nb, Q_BLOCK)))
    y_p = ob.swapaxes(0, 1).reshape(bp, sp, D_MODEL) @ w_o
    qs, ks, vs, logfs = fox_project(xs, w_in, b_f)
    k_all = jnp.concatenate([gather_pages(cache_k, page_table), ks], axis=1)
    v_all = jnp.concatenate([gather_pages(cache_v, page_table), vs], axis=1)
    c_all = jnp.cumsum(jnp.concatenate([gather_pages(cache_logf, page_table).astype(F32), logfs], axis=1), axis=1)
    k_pos = jnp.arange(past + ds, dtype=jnp.int32)
    o_s = fox_attend(qs, k_all, v_all, c_all[:, past:], c_all, k_pos[past:], k_pos)
    y_s = o_s.reshape(bs, ds, D_MODEL) @ w_o
    return y_p, y_s, (k, v, logf, ks, vs, logfs)


def nsa_project(x, pos, w_in, b_gate):
    b, t, _ = x.shape
    z = x @ w_in
    hq = NSA_HEADS * HEAD_DIM
    kvd = NSA_GROUPS * HEAD_DIM
    q = rope(z[..., :hq].reshape(b, t, NSA_HEADS, HEAD_DIM), pos)
    q = q.reshape(b, t, NSA_GROUPS, NSA_REP, HEAD_DIM)
    kv = z[..., hq:hq + 6 * kvd].reshape(b, t, 6, NSA_GROUPS, HEAD_DIM)
    rows = (rope(kv[:, :, 0], pos), kv[:, :, 1],
            rope(kv[:, :, 2], pos), kv[:, :, 3],
            rope(kv[:, :, 4], pos), kv[:, :, 5])
    gates = jax.nn.sigmoid((z[..., hq + 6 * kvd:] + b_gate).astype(F32)).reshape(b, t, 3, NSA_HEADS)
    return q, rows, gates


def nsa_compress(rows, pe, w1, w2):
    b, t, g, d = rows.shape
    nc = (t - NSA_CMP_LEN) // NSA_CMP_STRIDE + 1
    idx = np.arange(nc)[:, None] * NSA_CMP_STRIDE + np.arange(NSA_CMP_LEN)[None, :]
    blk = rows[:, idx] + pe[:, None, :]
    flat = jnp.moveaxis(blk, 3, 2).reshape(b, nc, g, NSA_CMP_LEN * d)
    return jax.nn.gelu(flat @ w1) @ w2


def nsa_overlap(nc, ns):
    c0 = np.arange(nc)[:, None] * NSA_CMP_STRIDE
    s0 = np.arange(ns)[None, :] * NSA_SEL_LEN
    ov = np.minimum(c0 + NSA_CMP_LEN, s0 + NSA_SEL_LEN) - np.maximum(c0, s0)
    return (np.clip(ov, 0, None) / NSA_CMP_STRIDE).astype(np.float32)


def nsa_cmp_branch(q, k_rows, v_rows, q_pos, pe_k, w1_k, w2_k, pe_v, w1_v, w2_v):
    t = k_rows.shape[1]
    kc = nsa_compress(k_rows, pe_k, w1_k, w2_k)
    vc = nsa_compress(v_rows, pe_v, w1_v, w2_v)
    nc = kc.shape[1]
    c_last = jnp.asarray(np.arange(nc) * NSA_CMP_STRIDE + NSA_CMP_LEN - 1, dtype=jnp.int32)
    s = jnp.einsum("bqgrd,bcgd->bgrqc", q, kc, preferred_element_type=F32) * SCALE
    p = masked_softmax(s, c_last[None, :] <= q_pos[:, None])
    o = jnp.einsum("bgrqc,bcgd->bqgrd", p.astype(vc.dtype), vc)
    ns = -(-t // NSA_SEL_LEN)
    imp = jnp.einsum("bgrqc,cj->bgqj", p, jnp.asarray(nsa_overlap(nc, ns)))
    return o, imp


def nsa_select(imp, q_pos):
    ns = imp.shape[-1]
    blk = jnp.arange(ns, dtype=jnp.int32)[None, :]
    cur = (q_pos // NSA_SEL_LEN)[:, None]
    valid = blk * NSA_SEL_LEN <= q_pos[:, None]
    forced = (blk == 0) | ((blk <= cur) & (blk > cur - NSA_N_LOCAL))
    score = jnp.where(forced, jnp.inf, jnp.where(valid, imp, -jnp.inf))
    return lax.top_k(score, min(NSA_TOP_N, ns))[1]


def nsa_sel_blocks(rows):
    b, t, g, d = rows.shape
    ns = -(-t // NSA_SEL_LEN)
    rows = jnp.pad(rows, ((0, 0), (0, ns * NSA_SEL_LEN - t), (0, 0), (0, 0)))
    return rows.reshape(b, ns, NSA_SEL_LEN, g, d).transpose(0, 3, 1, 2, 4)


def nsa_sel_attend(q, kbg, vbg, idx, q_pos):
    b, g = kbg.shape[:2]
    bi = jnp.arange(b)[:, None, None, None]
    gi = jnp.arange(g)[None, :, None, None]
    kg = kbg[bi, gi, idx]
    vg = vbg[bi, gi, idx]
    s = jnp.einsum("bqgrd,bgqkld->bgrqkl", q, kg, preferred_element_type=F32) * SCALE
    kpos = idx[..., None] * NSA_SEL_LEN + jnp.arange(NSA_SEL_LEN, dtype=jnp.int32)
    mask = kpos <= q_pos[None, None, :, None, None]
    shp = s.shape
    p = masked_softmax(s.reshape(shp[:4] + (-1,)), mask.reshape(b, g, 1, shp[3], -1)).reshape(shp)
    return jnp.einsum("bgrqkl,bgqkld->bqgrd", p.astype(vg.dtype), vg)


def nsa_window_attend(q, k, v, q_pos, k_pos):
    s = jnp.einsum("bqgrd,bkgd->bgrqk", q, k, preferred_element_type=F32) * SCALE
    dq = q_pos[:, None] - k_pos[None, :]
    mask = (dq >= 0) & (dq < NSA_WINDOW) & (k_pos[None, :] >= 0)
    p = masked_softmax(s, mask)
    return jnp.einsum("bgrqk,bkgd->bqgrd", p.astype(v.dtype), v)


def nsa_combine(o_cmp, o_sel, o_win, gates):
    b, t = o_cmp.shape[:2]
    g = gates.reshape(b, t, 3, NSA_GROUPS, NSA_REP, 1)
    o = g[:, :, 0] * o_cmp + g[:, :, 1] * o_sel + g[:, :, 2] * o_win
    return o.reshape(b, t, D_MODEL).astype(o_cmp.dtype)


def nsa_mixer(xp, xs, cache_cmp_k, cache_cmp_v, cache_sel_k, cache_sel_v, state_win_k, state_win_v,
              page_table, w_in, b_gate, pe_k, w1_k, w2_k, pe_v, w1_v, w2_v, w_o):
    bp, sp, _ = xp.shape
    bs, ds, _ = xs.shape
    past = page_table.shape[1] * cache_cmp_k.shape[1]
    wb = state_win_k.shape[1]
    pos = jnp.arange(sp, dtype=jnp.int32)
    q, (kc, vc, ksl, vsl, kw, vw), gates = nsa_project(xp, pos, w_in, b_gate)
    o_cmp, imp = nsa_cmp_branch(q, kc, vc, pos, pe_k, w1_k, w2_k, pe_v, w1_v, w2_v)
    idx = nsa_select(imp, pos)
    kbg, vbg = nsa_sel_blocks(ksl), nsa_sel_blocks(vsl)
    nch = sp // SEL_Q_BLOCK
    k_top = idx.shape[-1]
    o_sel = lax.map(lambda a: nsa_sel_attend(a[0], kbg, vbg, a[1], a[2]),
                    (q.reshape(bp, nch, SEL_Q_BLOCK, NSA_GROUPS, NSA_REP, HEAD_DIM).swapaxes(0, 1),
                     idx.reshape(bp, NSA_GROUPS, nch, SEL_Q_BLOCK, k_top).transpose(2, 0, 1, 3, 4),
                     pos.reshape(nch, SEL_Q_BLOCK)))
    o_sel = o_sel.swapaxes(0, 1).reshape(bp, sp, NSA_GROUPS, NSA_REP, HEAD_DIM)
    nb = sp // Q_BLOCK
    kidx = np.arange(nb)[:, None] * Q_BLOCK + np.arange(NSA_WINDOW + Q_BLOCK)[None, :]
    pad = ((0, 0), (NSA_WINDOW, 0), (0, 0), (0, 0))
    kwb = jnp.pad(kw, pad)[:, kidx].swapaxes(0, 1)
    vwb = jnp.pad(vw, pad)[:, kidx].swapaxes(0, 1)
    o_win = lax.map(lambda a: nsa_window_attend(*a),
                    (q.reshape(bp, nb, Q_BLOCK, NSA_GROUPS, NSA_REP, HEAD_DIM).swapaxes(0, 1),
                     kwb, vwb, pos.reshape(nb, Q_BLOCK),
                     jnp.asarray(kidx - NSA_WINDOW, dtype=jnp.int32)))
    o_win = o_win.swapaxes(0, 1).reshape(bp, sp, NSA_GROUPS, NSA_REP, HEAD_DIM)
    y_p = nsa_combine(o_cmp, o_sel, o_win, gates) @ w_o
    qpos = past + jnp.arange(ds, dtype=jnp.int32)
    qs, (kc_s, vc_s, ksl_s, vsl_s, kw_s, vw_s), gates_s = nsa_project(xs, qpos, w_in, b_gate)
    kc_all = jnp.concatenate([gather_pages(cache_cmp_k, page_table), kc_s], axis=1)
    vc_all = jnp.concatenate([gather_pages(cache_cmp_v, page_table), vc_s], axis=1)
    ksl_all = jnp.concatenate([gather_pages(cache_sel_k, page_table), ksl_s], axis=1)
    vsl_all = jnp.concatenate([gather_pages(cache_sel_v, page_table), vsl_s], axis=1)
    o_cmp_s, imp_s = nsa_cmp_branch(qs, kc_all, vc_all, qpos, pe_k, w1_k, w2_k, pe_v, w1_v, w2_v)
    idx_s = nsa_select(imp_s, qpos)
    o_sel_s = nsa_sel_attend(qs, nsa_sel_blocks(ksl_all), nsa_sel_blocks(vsl_all), idx_s, qpos)
    kw_all = jnp.concatenate([state_win_k, kw_s], axis=1)
    vw_all = jnp.concatenate([state_win_v, vw_s], axis=1)
    kw_pos = past - wb + jnp.arange(wb + ds, dtype=jnp.int32)
    o_win_s = nsa_window_attend(qs, kw_all, vw_all, qpos, kw_pos)
    y_s = nsa_combine(o_cmp_s, o_sel_s, o_win_s, gates_s) @ w_o
    keep = min(NSA_WINDOW, sp)
    return y_p, y_s, (kc, vc, ksl, vsl, kw[:, sp - keep:], vw[:, sp - keep:],
                      kc_s, vc_s, ksl_s, vsl_s, kw_all[:, ds:], vw_all[:, ds:])


def diff_project(x, pos, w_in):
    b, t, _ = x.shape
    q, k, v = jnp.split(x @ w_in, 3, axis=-1)
    q = rope(q.reshape(b, t, 2 * DIFF_HEADS, HEAD_DIM), pos)
    k = rope(k.reshape(b, t, 2 * DIFF_HEADS, HEAD_DIM), pos)
    v = v.reshape(b, t, DIFF_HEADS, 2 * HEAD_DIM)
    return q, k, v


def diff_attend(q, k, v, q_pos, k_pos, lam, lam_init, subln):
    b, tq = q.shape[:2]
    tk = k.shape[1]
    s = jnp.einsum("bqhd,bkhd->bhqk", q, k, preferred_element_type=F32) * SCALE
    p = masked_softmax(s, k_pos[None, :] <= q_pos[:, None]).reshape(b, DIFF_HEADS, 2, tq, tk)
    a = p[:, :, 0] - lam * p[:, :, 1]
    o = jnp.einsum("bhqk,bkhd->bqhd", a.astype(v.dtype), v)
    return rmsnorm(o, subln) * (1.0 - lam_init)


def diff_mixer(xp, xs, cache_k, cache_v, page_table, w_in, lq1, lk1, lq2, lk2, subln, w_o, layer_idx):
    lam_init = 0.8 - 0.6 * math.exp(-0.3 * layer_idx)
    lam = (jnp.exp(jnp.sum(lq1.astype(F32) * lk1.astype(F32)))
           - jnp.exp(jnp.sum(lq2.astype(F32) * lk2.astype(F32))) + lam_init)
    bp, sp, _ = xp.shape
    bs, ds, _ = xs.shape
    past = page_table.shape[1] * cache_k.shape[1]
    pos = jnp.arange(sp, dtype=jnp.int32)
    q, k, v = diff_project(xp, pos, w_in)
    nb = sp // Q_BLOCK
    ob = lax.map(lambda a: diff_attend(a[0], k, v, a[1], pos, lam, lam_init, subln),
                 (q.reshape(bp, nb, Q_BLOCK, 2 * DIFF_HEADS, HEAD_DIM).swapaxes(0, 1),
                  pos.reshape(nb, Q_BLOCK)))
    y_p = ob.swapaxes(0, 1).reshape(bp, sp, D_MODEL) @ w_o
    qpos = past + jnp.arange(ds, dtype=jnp.int32)
    qs, ks, vs = diff_project(xs, qpos, w_in)
    k_all = jnp.concatenate([gather_pages(cache_k, page_table), ks], axis=1)
    v_all = jnp.concatenate([gather_pages(cache_v, page_table), vs], axis=1)
    o_s = diff_attend(qs, k_all, v_all, qpos, jnp.arange(past + ds, dtype=jnp.int32), lam, lam_init, subln)
    y_s = o_s.reshape(bs, ds, D_MODEL) @ w_o
    return y_p, y_s, (k, v, ks, vs)


def setup_inputs(seed: int = 0) -> dict:
    key = jax.random.key(seed)
    ks = iter(jax.random.split(key, 96))

    def nrm(shape, scale=1.0):
        return jax.random.normal(next(ks), shape, F32) * scale

    d = D_MODEL
    dh = HEAD_DIM
    n_pages = PAST_LEN // PAGE_SIZE
    n_pool = POOL_NUM * DEC_BATCH * n_pages // POOL_DEN
    wb = min(NSA_WINDOW, PAST_LEN)
    gain = lambda n: 1.0 + nrm((n,), 0.02)
    nsa_in = NSA_HEADS * dh + 6 * NSA_GROUPS * dh + 3 * NSA_HEADS
    inp = {}
    inp["x_prompt"] = nrm((BATCH, SEQ, d))
    inp["x_sample"] = nrm((DEC_BATCH, DEC_SEQ, d))
    inp["cache_l0_k"] = nrm((n_pool, PAGE_SIZE, FOX_HEADS, dh))
    inp["cache_l0_v"] = nrm((n_pool, PAGE_SIZE, FOX_HEADS, dh))
    inp["cache_l0_logf"] = jax.nn.log_sigmoid(nrm((n_pool, PAGE_SIZE, FOX_HEADS)) + 2.5)
    inp["cache_l1_cmp_k"] = nrm((n_pool, PAGE_SIZE, NSA_GROUPS, dh))
    inp["cache_l1_cmp_v"] = nrm((n_pool, PAGE_SIZE, NSA_GROUPS, dh))
    inp["cache_l1_sel_k"] = nrm((n_pool, PAGE_SIZE, NSA_GROUPS, dh))
    inp["cache_l1_sel_v"] = nrm((n_pool, PAGE_SIZE, NSA_GROUPS, dh))
    inp["state_l1_win_k"] = nrm((DEC_BATCH, wb, NSA_GROUPS, dh))
    inp["state_l1_win_v"] = nrm((DEC_BATCH, wb, NSA_GROUPS, dh))
    inp["cache_l2_k"] = nrm((n_pool, PAGE_SIZE, 2 * DIFF_HEADS, dh))
    inp["cache_l2_v"] = nrm((n_pool, PAGE_SIZE, DIFF_HEADS, 2 * dh))
    inp["cache_l3_k"] = nrm((n_pool, PAGE_SIZE, FOX_HEADS, dh))
    inp["cache_l3_v"] = nrm((n_pool, PAGE_SIZE, FOX_HEADS, dh))
    inp["cache_l3_logf"] = jax.nn.log_sigmoid(nrm((n_pool, PAGE_SIZE, FOX_HEADS)) + 2.5)
    perm = jax.random.permutation(next(ks), n_pool)[: DEC_BATCH * n_pages]
    inp["page_table"] = perm.reshape(DEC_BATCH, n_pages).astype(jnp.int32)
    for l in (0, 1, 2, 3):
        p = "l%d_" % l
        inp[p + "norm_mix"] = gain(d)
        if l in (0, 3):
            inp[p + "fox_w_in"] = nrm((d, 3 * d + FOX_HEADS), d ** -0.5)
            inp[p + "fox_b_f"] = jax.random.uniform(next(ks), (FOX_HEADS,), F32, 1.0, 4.0)
            inp[p + "fox_w_o"] = nrm((d, d), d ** -0.5)
        elif l == 1:
            inp[p + "nsa_w_in"] = nrm((d, nsa_in), d ** -0.5)
            inp[p + "nsa_b_gate"] = nrm((3 * NSA_HEADS,), 0.1)
            inp[p + "nsa_pe_k"] = nrm((NSA_CMP_LEN, dh), 0.1)
            inp[p + "nsa_w1_k"] = nrm((NSA_CMP_LEN * dh, dh), (NSA_CMP_LEN * dh) ** -0.5)
            inp[p + "nsa_w2_k"] = nrm((dh, dh), dh ** -0.5)
            inp[p + "nsa_pe_v"] = nrm((NSA_CMP_LEN, dh), 0.1)
            inp[p + "nsa_w1_v"] = nrm((NSA_CMP_LEN * dh, dh), (NSA_CMP_LEN * dh) ** -0.5)
            inp[p + "nsa_w2_v"] = nrm((dh, dh), dh ** -0.5)
            inp[p + "nsa_w_o"] = nrm((d, d), d ** -0.5)
        else:
            inp[p + "diff_w_in"] = nrm((d, 3 * d), d ** -0.5)
            inp[p + "diff_lq1"] = nrm((dh,), 0.1)
            inp[p + "diff_lk1"] = nrm((dh,), 0.1)
            inp[p + "diff_lq2"] = nrm((dh,), 0.1)
            inp[p + "diff_lk2"] = nrm((dh,), 0.1)
            inp[p + "diff_subln"] = gain(2 * dh)
            inp[p + "diff_w_o"] = nrm((d, d), d ** -0.5)
        inp[p + "norm_mlp"] = gain(d)
        inp[p + "mlp_up"] = nrm((d, D_FF), d ** -0.5)
        inp[p + "mlp_down"] = nrm((D_FF, d), D_FF ** -0.5)
    inp["norm_final"] = gain(d)
    return inp


def reference(x_prompt, x_sample,
              cache_l0_k, cache_l0_v, cache_l0_logf,
              cache_l1_cmp_k, cache_l1_cmp_v, cache_l1_sel_k, cache_l1_sel_v,
              state_l1_win_k, state_l1_win_v,
              cache_l2_k, cache_l2_v,
              cache_l3_k, cache_l3_v, cache_l3_logf,
              page_table,
              l0_norm_mix, l0_fox_w_in, l0_fox_b_f, l0_fox_w_o, l0_norm_mlp, l0_mlp_up, l0_mlp_down,
              l1_norm_mix, l1_nsa_w_in, l1_nsa_b_gate, l1_nsa_pe_k, l1_nsa_w1_k, l1_nsa_w2_k,
              l1_nsa_pe_v, l1_nsa_w1_v, l1_nsa_w2_v, l1_nsa_w_o, l1_norm_mlp, l1_mlp_up, l1_mlp_down,
              l2_norm_mix, l2_diff_w_in, l2_diff_lq1, l2_diff_lk1, l2_diff_lq2, l2_diff_lk2,
              l2_diff_subln, l2_diff_w_o, l2_norm_mlp, l2_mlp_up, l2_mlp_down,
              l3_norm_mix, l3_fox_w_in, l3_fox_b_f, l3_fox_w_o, l3_norm_mlp, l3_mlp_up, l3_mlp_down,
              norm_final):
    mixers = (
        lambda hp, hs: fox_mixer(hp, hs, cache_l0_k, cache_l0_v, cache_l0_logf, page_table,
                                 l0_fox_w_in, l0_fox_b_f, l0_fox_w_o),
        lambda hp, hs: nsa_mixer(hp, hs, cache_l1_cmp_k, cache_l1_cmp_v, cache_l1_sel_k, cache_l1_sel_v,
                                 state_l1_win_k, state_l1_win_v, page_table,
                                 l1_nsa_w_in, l1_nsa_b_gate, l1_nsa_pe_k, l1_nsa_w1_k, l1_nsa_w2_k,
                                 l1_nsa_pe_v, l1_nsa_w1_v, l1_nsa_w2_v, l1_nsa_w_o),
        lambda hp, hs: diff_mixer(hp, hs, cache_l2_k, cache_l2_v, page_table, l2_diff_w_in,
                                  l2_diff_lq1, l2_diff_lk1, l2_diff_lq2, l2_diff_lk2,
                                  l2_diff_subln, l2_diff_w_o, DIFF_LAYER),
        lambda hp, hs: fox_mixer(hp, hs, cache_l3_k, cache_l3_v, cache_l3_logf, page_table,
                                 l3_fox_w_in, l3_fox_b_f, l3_fox_w_o),
    )
    norms_mix = (l0_norm_mix, l1_norm_mix, l2_norm_mix, l3_norm_mix)
    norms_mlp = (l0_norm_mlp, l1_norm_mlp, l2_norm_mlp, l3_norm_mlp)
    ups = (l0_mlp_up, l1_mlp_up, l2_mlp_up, l3_mlp_up)
    downs = (l0_mlp_down, l1_mlp_down, l2_mlp_down, l3_mlp_down)

    h_p, h_s = x_prompt, x_sample
    new_state = []
    for i in range(DEPTH):
        m_p, m_s, st = mixers[i](rmsnorm(h_p, norms_mix[i]), rmsnorm(h_s, norms_mix[i]))
        h_p = h_p + m_p.astype(h_p.dtype)
        h_s = h_s + m_s.astype(h_s.dtype)
        h_p = h_p + sqrelu_mlp(rmsnorm(h_p, norms_mlp[i]), ups[i], downs[i]).astype(h_p.dtype)
        h_s = h_s + sqrelu_mlp(rmsnorm(h_s, norms_mlp[i]), ups[i], downs[i]).astype(h_s.dtype)
        new_state.append(st)
    y_prompt = rmsnorm(h_p, norm_final)
    y_sample = rmsnorm(h_s, norm_final)

    (l0_k_p, l0_v_p, l0_lf_p, l0_k_s, l0_v_s, l0_lf_s) = new_state[0]
    (l1_ck_p, l1_cv_p, l1_sk_p, l1_sv_p, l1_wk_p, l1_wv_p,
     l1_ck_s, l1_cv_s, l1_sk_s, l1_sv_s, l1_wk_s, l1_wv_s) = new_state[1]
    (l2_k_p, l2_v_p, l2_k_s, l2_v_s) = new_state[2]
    (l3_k_p, l3_v_p, l3_lf_p, l3_k_s, l3_v_s, l3_lf_s) = new_state[3]
    return (y_prompt, y_sample,
            l0_k_p, l0_v_p, l0_lf_p, l0_k_s, l0_v_s, l0_lf_s,
            l1_ck_p, l1_cv_p, l1_sk_p, l1_sv_p, l1_wk_p, l1_wv_p,
            l1_ck_s, l1_cv_s, l1_sk_s, l1_sv_s, l1_wk_s, l1_wv_s,
            l2_k_p, l2_v_p, l2_k_s, l2_v_s,
            l3_k_p, l3_v_p, l3_lf_p, l3_k_s, l3_v_s, l3_lf_s)
```

```python
import functools
import math

import numpy as np
import jax
import jax.numpy as jnp
from jax import lax
from jax.experimental import pallas as pl
from jax.experimental.pallas import tpu as pltpu

F32 = jnp.float32
BF16 = jnp.bfloat16

D_MODEL = 1024
HEAD_DIM = 64
FOX_HEADS = 16
NSA_HEADS = 16
NSA_GROUPS = 4
NSA_REP = 4
NSA_CMP_LEN = 32
NSA_CMP_STRIDE = 16
NSA_SEL_LEN = 64
NSA_TOP_N = 16
NSA_N_LOCAL = 2
NSA_WINDOW = 512
DIFF_HEADS = 8
D_FF = 4096
ROPE_THETA = 10000.0
NORM_EPS = 1e-6
Q_BLOCK = 128
SEL_Q_BLOCK = 64
DIFF_LAYER = 2
SCALE = HEAD_DIM ** -0.5

VMEM_LIMIT = 56 * 1024 * 1024
ROW_TILE = 512


def _row_tile(m):
    return ROW_TILE if m % ROW_TILE == 0 else m


def _rms(x, g):
    return x * lax.rsqrt(jnp.mean(x * x, axis=-1, keepdims=True) + NORM_EPS) * g


def _norm_proj_kernel(x_ref, g_ref, *refs, n_out):
    xn = _rms(x_ref[...], g_ref[...]).astype(BF16)
    for w_ref, o_ref in zip(refs[:n_out], refs[n_out:]):
        o_ref[...] = jnp.dot(xn, w_ref[...], preferred_element_type=F32)


def norm_proj(x, g, ws):
    m, d = x.shape
    tm = _row_tile(m)
    n_out = len(ws)
    return pl.pallas_call(
        functools.partial(_norm_proj_kernel, n_out=n_out),
        grid=(m // tm,),
        in_specs=[pl.BlockSpec((tm, d), lambda i: (i, 0)),
                  pl.BlockSpec((1, d), lambda i: (0, 0))]
                 + [pl.BlockSpec(w.shape, lambda i: (0, 0)) for w in ws],
        out_specs=[pl.BlockSpec((tm, w.shape[1]), lambda i: (i, 0)) for w in ws],
        out_shape=[jax.ShapeDtypeStruct((m, w.shape[1]), F32) for w in ws],
        compiler_params=pltpu.CompilerParams(
            dimension_semantics=("arbitrary",), vmem_limit_bytes=VMEM_LIMIT),
        name="norm_proj",
    )(x, g.reshape(1, d), *ws)


def _out_proj_kernel(*refs, n_in):
    a_refs = refs[:n_in]
    w_ref, res_ref, o_ref = refs[n_in:]
    a = a_refs[0][...]
    for r in a_refs[1:]:
        a = a + r[...]
    o_ref[...] = res_ref[...] + jnp.dot(a.astype(BF16), w_ref[...], preferred_element_type=F32)


def out_proj(a_list, w, res):
    m, d = res.shape
    tm = _row_tile(m)
    k = w.shape[0]
    return pl.pallas_call(
        functools.partial(_out_proj_kernel, n_in=len(a_list)),
        grid=(m // tm,),
        in_specs=[pl.BlockSpec((tm, k), lambda i: (i, 0)) for _ in a_list]
                 + [pl.BlockSpec(w.shape, lambda i: (0, 0)),
                    pl.BlockSpec((tm, d), lambda i: (i, 0))],
        out_specs=pl.BlockSpec((tm, d), lambda i: (i, 0)),
        out_shape=jax.ShapeDtypeStruct((m, d), F32),
        compiler_params=pltpu.CompilerParams(
            dimension_semantics=("arbitrary",), vmem_limit_bytes=VMEM_LIMIT),
        name="out_proj",
    )(*a_list, w, res)


FF_CHUNK = 1024


def _mlp_kernel(x_ref, g_ref, wu_ref, wd_ref, o_ref):
    x = x_ref[...]
    xn = _rms(x, g_ref[...]).astype(BF16)
    acc = x
    for c in range(D_FF // FF_CHUNK):
        h = jnp.dot(xn, wu_ref[:, c * FF_CHUNK:(c + 1) * FF_CHUNK], preferred_element_type=F32)
        h = jnp.maximum(h, 0.0)
        acc = acc + jnp.dot((h * h).astype(BF16), wd_ref[c * FF_CHUNK:(c + 1) * FF_CHUNK, :],
                            preferred_element_type=F32)
    o_ref[...] = acc


def mlp(x, g, wu, wd):
    m, d = x.shape
    tm = _row_tile(m)
    return pl.pallas_call(
        _mlp_kernel,
        grid=(m // tm,),
        in_specs=[pl.BlockSpec((tm, d), lambda i: (i, 0)),
                  pl.BlockSpec((1, d), lambda i: (0, 0)),
                  pl.BlockSpec(wu.shape, lambda i: (0, 0)),
                  pl.BlockSpec(wd.shape, lambda i: (0, 0))],
        out_specs=pl.BlockSpec((tm, d), lambda i: (i, 0)),
        out_shape=jax.ShapeDtypeStruct((m, d), F32),
        compiler_params=pltpu.CompilerParams(
            dimension_semantics=("arbitrary",), vmem_limit_bytes=VMEM_LIMIT),
        name="mlp",
    )(x, g.reshape(1, d), wu, wd)


def _final_norm_kernel(x_ref, g_ref, o_ref):
    o_ref[...] = _rms(x_ref[...], g_ref[...])


def final_norm(x, g):
    m, d = x.shape
    tm = _row_tile(m)
    return pl.pallas_call(
        _final_norm_kernel,
        grid=(m // tm,),
        in_specs=[pl.BlockSpec((tm, d), lambda i: (i, 0)),
                  pl.BlockSpec((1, d), lambda i: (0, 0))],
        out_specs=pl.BlockSpec((tm, d), lambda i: (i, 0)),
        out_shape=jax.ShapeDtypeStruct((m, d), F32),
        compiler_params=pltpu.CompilerParams(dimension_semantics=("arbitrary",)),
        name="final_norm",
    )(x, g.reshape(1, d))


def rope(x, pos):
    half = x.shape[-1] // 2
    inv = ROPE_THETA ** (-jnp.arange(half, dtype=F32) / half)
    ang = pos.astype(F32)[:, None] * inv[None, :]
    cos = jnp.cos(ang)[:, None, :]
    sin = jnp.sin(ang)[:, None, :]
    xf = x.astype(F32)
    x1, x2 = xf[..., :half], xf[..., half:]
    return jnp.concatenate([x1 * cos - x2 * sin, x2 * cos + x1 * sin], axis=-1).astype(x.dtype)


def masked_softmax(logits, mask):
    logits = jnp.where(mask, logits.astype(F32), -jnp.inf)
    m = jnp.max(logits, axis=-1, keepdims=True)
    m = jnp.where(jnp.isfinite(m), m, 0.0)
    e = jnp.exp(logits - m)
    return e / jnp.maximum(jnp.sum(e, axis=-1, keepdims=True), 1e-30)


def gather_pages(pool, page_table):
    rows = pool[page_table]
    return rows.reshape((page_table.shape[0], page_table.shape[1] * pool.shape[1]) + pool.shape[2:])


def fox_attend(q, k, v, cq, ck, q_pos, k_pos):
    s = jnp.einsum("bqhd,bkhd->bhqk", q, k, preferred_element_type=F32) * SCALE
    s = s + jnp.swapaxes(cq, 1, 2)[..., :, None] - jnp.swapaxes(ck, 1, 2)[..., None, :]
    p = masked_softmax(s, k_pos[None, :] <= q_pos[:, None])
    return jnp.einsum("bhqk,bkhd->bqhd", p.astype(v.dtype), v)


def _pad_cols(w, n):
    return jnp.pad(w, ((0, 0), (0, n - w.shape[1])))


def fox_mixer(hp, hs, g, cache_k, cache_v, cache_logf, page_table, w_in, b_f, w_o):
    bp, sp, d = hp.shape
    bs, ds, _ = hs.shape
    past = page_table.shape[1] * cache_k.shape[1]
    hd = FOX_HEADS * HEAD_DIM
    wb = w_in.astype(BF16)
    ws = [wb[:, :hd], wb[:, hd:2 * hd], wb[:, 2 * hd:3 * hd], _pad_cols(wb[:, 3 * hd:], 128)]
    wo = w_o.astype(BF16)

    def project(x2d, b, t):
        q, k, v, f = norm_proj(x2d, g, ws)
        logf = jax.nn.log_sigmoid(f[:, :FOX_HEADS] + b_f).reshape(b, t, FOX_HEADS)
        sh = (b, t, FOX_HEADS, HEAD_DIM)
        return q.reshape(sh), k.reshape(sh), v.reshape(sh), logf

    hp2 = hp.reshape(bp * sp, d)
    hs2 = hs.reshape(bs * ds, d)
    q, k, v, logf = project(hp2, bp, sp)
    c = jnp.cumsum(logf, axis=1)
    pos = jnp.arange(sp, dtype=jnp.int32)
    nb = sp // Q_BLOCK
    ob = lax.map(lambda a: fox_attend(a[0], k, v, a[1], c, a[2], pos),
                 (q.reshape(bp, nb, Q_BLOCK, FOX_HEADS, HEAD_DIM).swapaxes(0, 1),
                  c.reshape(bp, nb, Q_BLOCK, FOX_HEADS).swapaxes(0, 1),
                  pos.reshape(nb, Q_BLOCK)))
    o_p = ob.swapaxes(0, 1).reshape(bp * sp, D_MODEL)
    hp_new = out_proj([o_p], wo, hp2).reshape(bp, sp, d)
    qs, ks, vs, logfs = project(hs2, bs, ds)
    k_all = jnp.concatenate([gather_pages(cache_k, page_table), ks], axis=1)
    v_all = jnp.concatenate([gather_pages(cache_v, page_table), vs], axis=1)
    c_all = jnp.cumsum(jnp.concatenate([gather_pages(cache_logf, page_table).astype(F32), logfs], axis=1), axis=1)
    k_pos = jnp.arange(past + ds, dtype=jnp.int32)
    o_s = fox_attend(qs, k_all, v_all, c_all[:, past:], c_all, k_pos[past:], k_pos)
    hs_new = out_proj([o_s.reshape(bs * ds, D_MODEL)], wo, hs2).reshape(bs, ds, d)
    return hp_new, hs_new, (k, v, logf, ks, vs, logfs)


def mlp_both(hp, hs, g, wu, wd):
    wu = wu.astype(BF16)
    wd = wd.astype(BF16)
    yp = mlp(hp.reshape(-1, D_MODEL), g, wu, wd).reshape(hp.shape)
    ys = mlp(hs.reshape(-1, D_MODEL), g, wu, wd).reshape(hs.shape)
    return yp, ys


def nsa_compress(rows, pe, w1, w2):
    b, t, g, d = rows.shape
    nc = (t - NSA_CMP_LEN) // NSA_CMP_STRIDE + 1
    idx = np.arange(nc)[:, None] * NSA_CMP_STRIDE + np.arange(NSA_CMP_LEN)[None, :]
    blk = rows[:, idx] + pe[:, None, :]
    flat = jnp.moveaxis(blk, 3, 2).reshape(b, nc, g, NSA_CMP_LEN * d)
    return jax.nn.gelu(flat @ w1) @ w2


def nsa_overlap(nc, ns):
    c0 = np.arange(nc)[:, None] * NSA_CMP_STRIDE
    s0 = np.arange(ns)[None, :] * NSA_SEL_LEN
    ov = np.minimum(c0 + NSA_CMP_LEN, s0 + NSA_SEL_LEN) - np.maximum(c0, s0)
    return (np.clip(ov, 0, None) / NSA_CMP_STRIDE).astype(np.float32)


def nsa_cmp_branch(q, k_rows, v_rows, q_pos, pe_k, w1_k, w2_k, pe_v, w1_v, w2_v):
    t = k_rows.shape[1]
    kc = nsa_compress(k_rows, pe_k, w1_k, w2_k)
    vc = nsa_compress(v_rows, pe_v, w1_v, w2_v)
    nc = kc.shape[1]
    c_last = jnp.asarray(np.arange(nc) * NSA_CMP_STRIDE + NSA_CMP_LEN - 1, dtype=jnp.int32)
    s = jnp.einsum("bqgrd,bcgd->bgrqc", q, kc, preferred_element_type=F32) * SCALE
    p = masked_softmax(s, c_last[None, :] <= q_pos[:, None])
    o = jnp.einsum("bgrqc,bcgd->bqgrd", p.astype(vc.dtype), vc)
    ns = -(-t // NSA_SEL_LEN)
    imp = jnp.einsum("bgrqc,cj->bgqj", p, jnp.asarray(nsa_overlap(nc, ns)))
    return o, imp


def nsa_select(imp, q_pos):
    ns = imp.shape[-1]
    blk = jnp.arange(ns, dtype=jnp.int32)[None, :]
    cur = (q_pos // NSA_SEL_LEN)[:, None]
    valid = blk * NSA_SEL_LEN <= q_pos[:, None]
    forced = (blk == 0) | ((blk <= cur) & (blk > cur - NSA_N_LOCAL))
    score = jnp.where(forced, jnp.inf, jnp.where(valid, imp, -jnp.inf))
    return lax.top_k(score, min(NSA_TOP_N, ns))[1]


def nsa_sel_blocks(rows):
    b, t, g, d = rows.shape
    ns = -(-t // NSA_SEL_LEN)
    rows = jnp.pad(rows, ((0, 0), (0, ns * NSA_SEL_LEN - t), (0, 0), (0, 0)))
    return rows.reshape(b, ns, NSA_SEL_LEN, g, d).transpose(0, 3, 1, 2, 4)


def nsa_sel_attend(q, kbg, vbg, idx, q_pos):
    b, g = kbg.shape[:2]
    bi = jnp.arange(b)[:, None, None, None]
    gi = jnp.arange(g)[None, :, None, None]
    kg = kbg[bi, gi, idx]
    vg = vbg[bi, gi, idx]
    s = jnp.einsum("bqgrd,bgqkld->bgrqkl", q, kg, preferred_element_type=F32) * SCALE
    kpos = idx[..., None] * NSA_SEL_LEN + jnp.arange(NSA_SEL_LEN, dtype=jnp.int32)
    mask = kpos <= q_pos[None, None, :, None, None]
    shp = s.shape
    p = masked_softmax(s.reshape(shp[:4] + (-1,)), mask.reshape(b, g, 1, shp[3], -1)).reshape(shp)
    return jnp.einsum("bgrqkl,bgqkld->bqgrd", p.astype(vg.dtype), vg)


def nsa_window_attend(q, k, v, q_pos, k_pos):
    s = jnp.einsum("bqgrd,bkgd->bgrqk", q, k, preferred_element_type=F32) * SCALE
    dq = q_pos[:, None] - k_pos[None, :]
    mask = (dq >= 0) & (dq < NSA_WINDOW) & (k_pos[None, :] >= 0)
    p = masked_softmax(s, mask)
    return jnp.einsum("bgrqk,bkgd->bqgrd", p.astype(v.dtype), v)


def nsa_combine(o_cmp, o_sel, o_win, gates):
    b, t = o_cmp.shape[:2]
    g = gates.reshape(b, t, 3, NSA_GROUPS, NSA_REP, 1)
    o = g[:, :, 0] * o_cmp + g[:, :, 1] * o_sel + g[:, :, 2] * o_win
    return o.reshape(b, t, D_MODEL).astype(o_cmp.dtype)


def nsa_mixer(hp, hs, gn, cache_cmp_k, cache_cmp_v, cache_sel_k, cache_sel_v, state_win_k, state_win_v,
              page_table, w_in, b_gate, pe_k, w1_k, w2_k, pe_v, w1_v, w2_v, w_o):
    bp, sp, d = hp.shape
    bs, ds, _ = hs.shape
    past = page_table.shape[1] * cache_cmp_k.shape[1]
    wb_ = state_win_k.shape[1]
    hq = NSA_HEADS * HEAD_DIM
    kvd = NSA_GROUPS * HEAD_DIM
    wbf = w_in.astype(BF16)
    ws = [wbf[:, :hq], wbf[:, hq:hq + 6 * kvd], _pad_cols(wbf[:, hq + 6 * kvd:], 128)]
    wo = w_o.astype(BF16)

    def project(x2d, b, t, pos):
        q, kv, gt = norm_proj(x2d, gn, ws)
        q = rope(q.reshape(b, t, NSA_HEADS, HEAD_DIM), pos).reshape(b, t, NSA_GROUPS, NSA_REP, HEAD_DIM)
        kv = kv.reshape(b, t, 6, NSA_GROUPS, HEAD_DIM)
        rows = (rope(kv[:, :, 0], pos), kv[:, :, 1], rope(kv[:, :, 2], pos), kv[:, :, 3],
                rope(kv[:, :, 4], pos), kv[:, :, 5])
        gates = jax.nn.sigmoid(gt[:, :3 * NSA_HEADS] + b_gate).reshape(b, t, 3, NSA_HEADS)
        return q, rows, gates

    hp2 = hp.reshape(bp * sp, d)
    hs2 = hs.reshape(bs * ds, d)
    pos = jnp.arange(sp, dtype=jnp.int32)
    q, (kc, vc, ksl, vsl, kw, vw), gates = project(hp2, bp, sp, pos)
    o_cmp, imp = nsa_cmp_branch(q, kc, vc, pos, pe_k, w1_k, w2_k, pe_v, w1_v, w2_v)
    idx = nsa_select(imp, pos)
    kbg, vbg = nsa_sel_blocks(ksl), nsa_sel_blocks(vsl)
    nch = sp // SEL_Q_BLOCK
    k_top = idx.shape[-1]
    o_sel = lax.map(lambda a: nsa_sel_attend(a[0], kbg, vbg, a[1], a[2]),
                    (q.reshape(bp, nch, SEL_Q_BLOCK, NSA_GROUPS, NSA_REP, HEAD_DIM).swapaxes(0, 1),
                     idx.reshape(bp, NSA_GROUPS, nch, SEL_Q_BLOCK, k_top).transpose(2, 0, 1, 3, 4),
                     pos.reshape(nch, SEL_Q_BLOCK)))
    o_sel = o_sel.swapaxes(0, 1).reshape(bp, sp, NSA_GROUPS, NSA_REP, HEAD_DIM)
    nb = sp // Q_BLOCK
    kidx = np.arange(nb)[:, None] * Q_BLOCK + np.arange(NSA_WINDOW + Q_BLOCK)[None, :]
    pad = ((0, 0), (NSA_WINDOW, 0), (0, 0), (0, 0))
    kwb = jnp.pad(kw, pad)[:, kidx].swapaxes(0, 1)
    vwb = jnp.pad(vw, pad)[:, kidx].swapaxes(0, 1)
    o_win = lax.map(lambda a: nsa_window_attend(*a),
                    (q.reshape(bp, nb, Q_BLOCK, NSA_GROUPS, NSA_REP, HEAD_DIM).swapaxes(0, 1),
                     kwb, vwb, pos.reshape(nb, Q_BLOCK),
                     jnp.asarray(kidx - NSA_WINDOW, dtype=jnp.int32)))
    o_win = o_win.swapaxes(0, 1).reshape(bp, sp, NSA_GROUPS, NSA_REP, HEAD_DIM)
    hp_new = out_proj([nsa_combine(o_cmp, o_sel, o_win, gates).reshape(bp * sp, d)], wo, hp2).reshape(bp, sp, d)
    qpos = past + jnp.arange(ds, dtype=jnp.int32)
    qs, (kc_s, vc_s, ksl_s, vsl_s, kw_s, vw_s), gates_s = project(hs2, bs, ds, qpos)
    kc_all = jnp.concatenate([gather_pages(cache_cmp_k, page_table), kc_s], axis=1)
    vc_all = jnp.concatenate([gather_pages(cache_cmp_v, page_table), vc_s], axis=1)
    ksl_all = jnp.concatenate([gather_pages(cache_sel_k, page_table), ksl_s], axis=1)
    vsl_all = jnp.concatenate([gather_pages(cache_sel_v, page_table), vsl_s], axis=1)
    o_cmp_s, imp_s = nsa_cmp_branch(qs, kc_all, vc_all, qpos, pe_k, w1_k, w2_k, pe_v, w1_v, w2_v)
    idx_s = nsa_select(imp_s, qpos)
    o_sel_s = nsa_sel_attend(qs, nsa_sel_blocks(ksl_all), nsa_sel_blocks(vsl_all), idx_s, qpos)
    kw_all = jnp.concatenate([state_win_k, kw_s], axis=1)
    vw_all = jnp.concatenate([state_win_v, vw_s], axis=1)
    kw_pos = past - wb_ + jnp.arange(wb_ + ds, dtype=jnp.int32)
    o_win_s = nsa_window_attend(qs, kw_all, vw_all, qpos, kw_pos)
    hs_new = out_proj([nsa_combine(o_cmp_s, o_sel_s, o_win_s, gates_s).reshape(bs * ds, d)], wo, hs2).reshape(bs, ds, d)
    keep = min(NSA_WINDOW, sp)
    return hp_new, hs_new, (kc, vc, ksl, vsl, kw[:, sp - keep:], vw[:, sp - keep:],
                            kc_s, vc_s, ksl_s, vsl_s, kw_all[:, ds:], vw_all[:, ds:])


def _rmsnorm_ref(x, g):
    y = x * lax.rsqrt(jnp.mean(x * x, axis=-1, keepdims=True) + NORM_EPS)
    return y * g


def diff_attend(q, k, v, q_pos, k_pos, lam, lam_init, subln):
    b, tq = q.shape[:2]
    tk = k.shape[1]
    s = jnp.einsum("bqhd,bkhd->bhqk", q, k, preferred_element_type=F32) * SCALE
    p = masked_softmax(s, k_pos[None, :] <= q_pos[:, None]).reshape(b, DIFF_HEADS, 2, tq, tk)
    a = p[:, :, 0] - lam * p[:, :, 1]
    o = jnp.einsum("bhqk,bkhd->bqhd", a.astype(v.dtype), v)
    return _rmsnorm_ref(o, subln) * (1.0 - lam_init)


def diff_mixer(hp, hs, gn, cache_k, cache_v, page_table, w_in, lq1, lk1, lq2, lk2, subln, w_o, layer_idx):
    lam_init = 0.8 - 0.6 * math.exp(-0.3 * layer_idx)
    lam = (jnp.exp(jnp.sum(lq1.astype(F32) * lk1.astype(F32)))
           - jnp.exp(jnp.sum(lq2.astype(F32) * lk2.astype(F32))) + lam_init)
    bp, sp, d = hp.shape
    bs, ds, _ = hs.shape
    past = page_table.shape[1] * cache_k.shape[1]
    wbf = w_in.astype(BF16)
    ws = [wbf[:, :d], wbf[:, d:2 * d], wbf[:, 2 * d:]]
    wo = w_o.astype(BF16)

    def project(x2d, b, t, pos):
        q, k, v = norm_proj(x2d, gn, ws)
        q = rope(q.reshape(b, t, 2 * DIFF_HEADS, HEAD_DIM), pos)
        k = rope(k.reshape(b, t, 2 * DIFF_HEADS, HEAD_DIM), pos)
        v = v.reshape(b, t, DIFF_HEADS, 2 * HEAD_DIM)
        return q, k, v

    hp2 = hp.reshape(bp * sp, d)
    hs2 = hs.reshape(bs * ds, d)
    pos = jnp.arange(sp, dtype=jnp.int32)
    q, k, v = project(hp2, bp, sp, pos)
    nb = sp // Q_BLOCK
    ob = lax.map(lambda a: diff_attend(a[0], k, v, a[1], pos, lam, lam_init, subln),
                 (q.reshape(bp, nb, Q_BLOCK, 2 * DIFF_HEADS, HEAD_DIM).swapaxes(0, 1),
                  pos.reshape(nb, Q_BLOCK)))
    hp_new = out_proj([ob.swapaxes(0, 1).reshape(bp * sp, D_MODEL)], wo, hp2).reshape(bp, sp, d)
    qpos = past + jnp.arange(ds, dtype=jnp.int32)
    qs, ks, vs = project(hs2, bs, ds, qpos)
    k_all = jnp.concatenate([gather_pages(cache_k, page_table), ks], axis=1)
    v_all = jnp.concatenate([gather_pages(cache_v, page_table), vs], axis=1)
    o_s = diff_attend(qs, k_all, v_all, qpos, jnp.arange(past + ds, dtype=jnp.int32), lam, lam_init, subln)
    hs_new = out_proj([o_s.reshape(bs * ds, D_MODEL)], wo, hs2).reshape(bs, ds, d)
    return hp_new, hs_new, (k, v, ks, vs)


def kernel(x_prompt,x_sample, cache_l0_k, cache_l0_v, cache_l0_logf, cache_l1_cmp_k, cache_l1_cmp_v, cache_l1_sel_k, cache_l1_sel_v, state_l1_win_k, state_l1_win_v, cache_l2_k, cache_l2_v, cache_l3_k, cache_l3_v, cache_l3_logf, page_table, l0_norm_mix, l0_fox_w_in, l0_fox_b_f, l0_fox_w_o, l0_norm_mlp, l0_mlp_up, l0_mlp_down, l1_norm_mix, l1_nsa_w_in, l1_nsa_b_gate, l1_nsa_pe_k, l1_nsa_w1_k, l1_nsa_w2_k, l1_nsa_pe_v, l1_nsa_w1_v, l1_nsa_w2_v, l1_nsa_w_o, l1_norm_mlp, l1_mlp_up, l1_mlp_down, l2_norm_mix, l2_diff_w_in, l2_diff_lq1, l2_diff_lk1, l2_diff_lq2, l2_diff_lk2, l2_diff_subln, l2_diff_w_o, l2_norm_mlp, l2_mlp_up, l2_mlp_down, l3_norm_mix, l3_fox_w_in, l3_fox_b_f, l3_fox_w_o, l3_norm_mlp, l3_mlp_up, l3_mlp_down, norm_final):
    hp, hs = x_prompt, x_sample
    state = []
    hp, hs, st = fox_mixer(hp, hs, l0_norm_mix, cache_l0_k, cache_l0_v, cache_l0_logf, page_table,
                           l0_fox_w_in, l0_fox_b_f, l0_fox_w_o)
    state += st
    hp, hs = mlp_both(hp, hs, l0_norm_mlp, l0_mlp_up, l0_mlp_down)
    hp, hs, st = nsa_mixer(hp, hs, l1_norm_mix, cache_l1_cmp_k, cache_l1_cmp_v, cache_l1_sel_k, cache_l1_sel_v,
                           state_l1_win_k, state_l1_win_v, page_table,
                           l1_nsa_w_in, l1_nsa_b_gate, l1_nsa_pe_k, l1_nsa_w1_k, l1_nsa_w2_k,
                           l1_nsa_pe_v, l1_nsa_w1_v, l1_nsa_w2_v, l1_nsa_w_o)
    state += st
    hp, hs = mlp_both(hp, hs, l1_norm_mlp, l1_mlp_up, l1_mlp_down)
    hp, hs, st = diff_mixer(hp, hs, l2_norm_mix, cache_l2_k, cache_l2_v, page_table, l2_diff_w_in,
                            l2_diff_lq1, l2_diff_lk1, l2_diff_lq2, l2_diff_lk2,
                            l2_diff_subln, l2_diff_w_o, DIFF_LAYER)
    state += st
    hp, hs = mlp_both(hp, hs, l2_norm_mlp, l2_mlp_up, l2_mlp_down)
    hp, hs, st = fox_mixer(hp, hs, l3_norm_mix, cache_l3_k, cache_l3_v, cache_l3_logf, page_table,
                           l3_fox_w_in, l3_fox_b_f, l3_fox_w_o)
    state += st
    hp, hs = mlp_both(hp, hs, l3_norm_mlp, l3_mlp_up, l3_mlp_down)
    y_p = final_norm(hp.reshape(-1, D_MODEL), norm_final).reshape(hp.shape)
    y_s = final_norm(hs.reshape(-1, D_MODEL), norm_final).reshape(hs.shape)
    return (y_p, y_s) + tuple(state)
```

```python
import functools
import math
from typing import NamedTuple

import numpy as np
import jax
import jax.numpy as jnp
from jax import lax
from jax.experimental import pallas as pl
from jax.experimental.pallas import tpu as pltpu

F32 = jnp.float32
BF16 = jnp.bfloat16

D_MODEL = 1024
HEAD_DIM = 64
FOX_HEADS = 16
NSA_HEADS = 16
NSA_GROUPS = 4
NSA_REP = 4
NSA_CMP_LEN = 32
NSA_CMP_STRIDE = 16
NSA_SEL_LEN = 64
NSA_TOP_N = 16
NSA_N_LOCAL = 2
NSA_WINDOW = 512
DIFF_HEADS = 8
D_FF = 4096
ROPE_THETA = 10000.0
NORM_EPS = 1e-6
DIFF_LAYER = 2
SCALE = HEAD_DIM ** -0.5

LANES = 128
VMEM_LIMIT = 56 * 1024 * 1024
ROW_TILE = 512
NEG = -1e30

_ARB = lambda n: pltpu.CompilerParams(dimension_semantics=("arbitrary",) * n,
                                      vmem_limit_bytes=VMEM_LIMIT)


def _row_tile(m):
    return ROW_TILE if m % ROW_TILE == 0 else m


def _rms(x, g):
    return x * lax.rsqrt(jnp.mean(x * x, axis=-1, keepdims=True) + NORM_EPS) * g


def _swap_halves(x):
    parts = []
    for c in range(x.shape[-1] // LANES):
        parts += [x[:, c * LANES + 64:(c + 1) * LANES], x[:, c * LANES:c * LANES + 64]]
    return jnp.concatenate(parts, axis=-1)


class ProjCfg(NamedTuple):
    rope: bool = False
    f32: bool = True
    bf16: bool = False
    scale: float = 1.0
    act: str = ""


def _rope(y, cos, sin):
    n = y.shape[-1]
    reps = n // LANES
    if reps > 1:
        cos = jnp.concatenate([cos] * reps, axis=-1)
        sin = jnp.concatenate([sin] * reps, axis=-1)
    lane = lax.broadcasted_iota(jnp.int32, y.shape, 1)
    partner = jnp.where((lane % HEAD_DIM) < HEAD_DIM // 2,
                        pltpu.roll(y, n - HEAD_DIM // 2, 1), pltpu.roll(y, HEAD_DIM // 2, 1))
    return y * cos + partner * sin


def _norm_proj_kernel(*refs, cfgs, use_rope):
    it = iter(refs)
    x_ref, g_ref = next(it), next(it)
    if use_rope:
        cos_ref, sin_ref = next(it), next(it)
    ins = []
    for c in cfgs:
        w_ref = next(it)
        ins.append((w_ref, next(it) if c.act else None))
    xn = _rms(x_ref[...], g_ref[...]).astype(BF16)
    for c, (w_ref, b_ref) in zip(cfgs, ins):
        y = jnp.dot(xn, w_ref[...], preferred_element_type=F32)
        if c.rope:
            y = _rope(y, cos_ref[...], sin_ref[...])
        if c.act:
            z = y + b_ref[...]
            if c.act == "logsig":
                y = jnp.minimum(z, 0.0) - jnp.log1p(jnp.exp(-jnp.abs(z)))
            else:
                y = 1.0 / (1.0 + jnp.exp(-z))
        if c.f32:
            next(it)[...] = y
        if c.bf16:
            next(it)[...] = (y * c.scale).astype(BF16)


def norm_proj(x, g, ws, cfgs, biases=None, rope_tables=None):
    m, d = x.shape
    tm = _row_tile(m)
    use_rope = rope_tables is not None
    args = [x, g.reshape(1, d)]
    in_specs = [pl.BlockSpec((tm, d), lambda i: (i, 0)), pl.BlockSpec((1, d), lambda i: (0, 0))]
    if use_rope:
        nt = rope_tables[0].shape[0] // tm
        for t in rope_tables:
            args.append(t)
            in_specs.append(pl.BlockSpec((tm, LANES), lambda i: (i % nt, 0)))
    out_specs, out_shape = [], []
    for ci, (w, c) in enumerate(zip(ws, cfgs)):
        n = w.shape[1]
        args.append(w)
        in_specs.append(pl.BlockSpec(w.shape, lambda i: (0, 0)))
        if c.act:
            args.append(biases[ci].reshape(1, n))
            in_specs.append(pl.BlockSpec((1, n), lambda i: (0, 0)))
        for flag, dt in ((c.f32, F32), (c.bf16, BF16)):
            if flag:
                out_specs.append(pl.BlockSpec((tm, n), lambda i: (i, 0)))
                out_shape.append(jax.ShapeDtypeStruct((m, n), dt))
    return pl.pallas_call(
        functools.partial(_norm_proj_kernel, cfgs=tuple(cfgs), use_rope=use_rope),
        grid=(m // tm,), in_specs=in_specs, out_specs=out_specs, out_shape=out_shape,
        compiler_params=_ARB(1), name="norm_proj",
    )(*args)


def _out_proj_kernel(*refs, n_in):
    a_refs = refs[:n_in]
    w_ref, res_ref, o_ref = refs[n_in:]
    a = a_refs[0][...]
    for r in a_refs[1:]:
        a = a + r[...]
    o_ref[...] = res_ref[...] + jnp.dot(a.astype(BF16), w_ref[...], preferred_element_type=F32)


def out_proj(a_list, w, res):
    m, d = res.shape
    tm = _row_tile(m)
    k = w.shape[0]
    return pl.pallas_call(
        functools.partial(_out_proj_kernel, n_in=len(a_list)),
        grid=(m // tm,),
        in_specs=[pl.BlockSpec((tm, k), lambda i: (i, 0)) for _ in a_list]
                 + [pl.BlockSpec(w.shape, lambda i: (0, 0)),
                    pl.BlockSpec((tm, d), lambda i: (i, 0))],
        out_specs=pl.BlockSpec((tm, d), lambda i: (i, 0)),
        out_shape=jax.ShapeDtypeStruct((m, d), F32),
        compiler_params=_ARB(1), name="out_proj",
    )(*a_list, w, res)


FF_CHUNK = 1024


def _mlp_kernel(x_ref, g_ref, wu_ref, wd_ref, o_ref):
    x = x_ref[...]
    xn = _rms(x, g_ref[...]).astype(BF16)
    acc = x
    for c in range(D_FF // FF_CHUNK):
        h = jnp.dot(xn, wu_ref[:, c * FF_CHUNK:(c + 1) * FF_CHUNK], preferred_element_type=F32)
        h = jnp.maximum(h, 0.0)
        acc = acc + jnp.dot((h * h).astype(BF16), wd_ref[c * FF_CHUNK:(c + 1) * FF_CHUNK, :],
                            preferred_element_type=F32)
    o_ref[...] = acc


def mlp(x, g, wu, wd):
    m, d = x.shape
    tm = _row_tile(m)
    return pl.pallas_call(
        _mlp_kernel,
        grid=(m // tm,),
        in_specs=[pl.BlockSpec((tm, d), lambda i: (i, 0)),
                  pl.BlockSpec((1, d), lambda i: (0, 0)),
                  pl.BlockSpec(wu.shape, lambda i: (0, 0)),
                  pl.BlockSpec(wd.shape, lambda i: (0, 0))],
        out_specs=pl.BlockSpec((tm, d), lambda i: (i, 0)),
        out_shape=jax.ShapeDtypeStruct((m, d), F32),
        compiler_params=_ARB(1), name="mlp",
    )(x, g.reshape(1, d), wu, wd)


def _final_norm_kernel(x_ref, g_ref, o_ref):
    o_ref[...] = _rms(x_ref[...], g_ref[...])


def final_norm(x, g):
    m, d = x.shape
    tm = _row_tile(m)
    return pl.pallas_call(
        _final_norm_kernel,
        grid=(m // tm,),
        in_specs=[pl.BlockSpec((tm, d), lambda i: (i, 0)),
                  pl.BlockSpec((1, d), lambda i: (0, 0))],
        out_specs=pl.BlockSpec((tm, d), lambda i: (i, 0)),
        out_shape=jax.ShapeDtypeStruct((m, d), F32),
        compiler_params=_ARB(1), name="final_norm",
    )(x, g.reshape(1, d))


def rope_tables(pos):
    half = HEAD_DIM // 2
    inv = ROPE_THETA ** (-jnp.arange(half, dtype=F32) / half)
    ang = pos.astype(F32)[:, None] * inv[None, :]
    cos, sin = jnp.cos(ang), jnp.sin(ang)
    cos = jnp.concatenate([cos, cos, cos, cos], axis=-1)
    sin = jnp.concatenate([-sin, sin, -sin, sin], axis=-1)
    return cos, sin


def _flash_chunks(q, load_kv, j_lo, j_diag, bias_fn, mask_fn, robust):
    rows = q.shape[0]

    def step(j, carry, diag):
        m, l, acc = carry
        k, v = load_kv(j)
        s = lax.dot_general(q, k, (((1,), (1,)), ((), ())), preferred_element_type=F32)
        if bias_fn is not None:
            s = s + bias_fn(j)
        msk = mask_fn(j, diag) if mask_fn is not None else None
        if msk is not None:
            s = jnp.where(msk, s, NEG)
        m_new = jnp.maximum(m, jnp.max(s, axis=-1, keepdims=True))
        alpha = jnp.exp(m - m_new)
        p = jnp.exp(s - m_new)
        if msk is not None and robust:
            p = jnp.where(msk, p, 0.0)
        l = alpha * l + jnp.sum(p, axis=-1, keepdims=True)
        acc = alpha * acc + jnp.dot(p.astype(BF16), v, preferred_element_type=F32)
        return m_new, l, acc

    dv = load_kv(0)[1].shape[-1]
    init = (jnp.full((rows, 1), NEG, F32), jnp.zeros((rows, 1), F32), jnp.zeros((rows, dv), F32))
    carry = lax.fori_loop(j_lo, j_diag, lambda j, c: step(j, c, False), init)
    _, l, acc = step(j_diag, carry, True)
    return acc / jnp.maximum(l, 1e-30)


def _causal_mask(q0, tq, rows, j, tk):
    qpos = q0 + lax.broadcasted_iota(jnp.int32, (rows, tk), 0) % tq
    kpos = j * tk + lax.broadcasted_iota(jnp.int32, (rows, tk), 1)
    return qpos, kpos


def _head_pair_kernel(*refs, tq, tk, mode, lam_init):
    if mode == "fox":
        q_ref, k_ref, v_ref, cq_ref, ck_ref, o_ref = refs
    else:
        q_ref, k_ref, v_ref, lam_ref, subln_ref, o_ref = refs
    q0 = pl.program_id(2) * tq
    q = q_ref[0]
    lane = lax.broadcasted_iota(jnp.int32, (tq, LANES), 1)
    zero = jnp.zeros_like(q)

    def load_kv(j):
        st = pl.multiple_of(j * tk, tk)
        return k_ref[0, pl.ds(st, tk), :], v_ref[0, pl.ds(st, tk), :]

    def mask_fn(j, diag):
        if not diag:
            return None
        qpos, kpos = _causal_mask(q0, tq, tq, j, tk)
        return kpos <= qpos

    outs = []
    for i in range(2):
        qm = jnp.where((lane >= 64 * i) & (lane < 64 * (i + 1)), q, zero)
        if mode == "fox":
            cq = cq_ref[0, 0][:, i:i + 1]
            bias_fn = lambda j, cq=cq, i=i: cq - ck_ref[0, 0, j, i:i + 1, :]
        else:
            bias_fn = None
        outs.append(_flash_chunks(qm, load_kv, 0, q0 // tk, bias_fn, mask_fn, robust=False))
    if mode == "fox":
        o = jnp.where(lane < 64, outs[0], outs[1])
    else:
        o = outs[0] - lam_ref[...] * outs[1]
        o = _rms(o, subln_ref[...]) * (1.0 - lam_init)
    o_ref[0] = o.astype(o_ref.dtype)


ATT_TQ = 256
ATT_TK = 512


def fox_prompt_attention(q, k, v, c):
    b, s, d = q.shape
    tq, tk = min(ATT_TQ, s), min(ATT_TK, s)
    hp = d // LANES
    cq = c.reshape(b, s, hp, 2).transpose(0, 2, 1, 3)
    ck = c.reshape(b, s // tk, tk, hp, 2).transpose(0, 3, 1, 4, 2)
    return pl.pallas_call(
        functools.partial(_head_pair_kernel, tq=tq, tk=tk, mode="fox", lam_init=0.0),
        grid=(b, hp, s // tq),
        in_specs=[pl.BlockSpec((1, tq, LANES), lambda bi, h, qi: (bi, qi, h)),
                  pl.BlockSpec((1, s, LANES), lambda bi, h, qi: (bi, 0, h)),
                  pl.BlockSpec((1, s, LANES), lambda bi, h, qi: (bi, 0, h)),
                  pl.BlockSpec((1, 1, tq, 2), lambda bi, h, qi: (bi, h, qi, 0)),
                  pl.BlockSpec((1, 1, s // tk, 2, tk), lambda bi, h, qi: (bi, h, 0, 0, 0))],
        out_specs=pl.BlockSpec((1, tq, LANES), lambda bi, h, qi: (bi, qi, h)),
        out_shape=jax.ShapeDtypeStruct((b, s, d), BF16),
        compiler_params=_ARB(3), name="fox_prompt_attention",
    )(q, k, v, cq, ck)


def diff_prompt_attention(q, k, v, lam, lam_init, subln):
    b, s, d = q.shape
    tq, tk = min(ATT_TQ, s), min(ATT_TK, s)
    hp = d // LANES
    lam_vec = jnp.broadcast_to(lam.astype(F32).reshape(1, 1), (1, LANES))
    return pl.pallas_call(
        functools.partial(_head_pair_kernel, tq=tq, tk=tk, mode="diff", lam_init=lam_init),
        grid=(b, hp, s // tq),
        in_specs=[pl.BlockSpec((1, tq, LANES), lambda bi, h, qi: (bi, qi, h)),
                  pl.BlockSpec((1, s, LANES), lambda bi, h, qi: (bi, 0, h)),
                  pl.BlockSpec((1, s, LANES), lambda bi, h, qi: (bi, 0, h)),
                  pl.BlockSpec((1, LANES), lambda bi, h, qi: (0, 0)),
                  pl.BlockSpec((1, LANES), lambda bi, h, qi: (0, 0))],
        out_specs=pl.BlockSpec((1, tq, LANES), lambda bi, h, qi: (bi, qi, h)),
        out_shape=jax.ShapeDtypeStruct((b, s, d), BF16),
        compiler_params=_ARB(3), name="diff_prompt_attention",
    )(q, k, v, lam_vec, subln.reshape(1, LANES))


def _compress_core(get_phase, pe_ref, w1_ref, w2_ref, nc):
    lo = jnp.zeros((nc, 4 * HEAD_DIM), F32)
    hi = jnp.zeros((nc, 4 * HEAD_DIM), F32)
    for l in range(NSA_CMP_STRIDE):
        ph = get_phase(l)
        lo = lo + jnp.dot((ph + pe_ref[l:l + 1, :]).astype(BF16), w1_ref[l],
                          preferred_element_type=F32)
        hi = hi + jnp.dot((ph + pe_ref[l + NSA_CMP_STRIDE:l + NSA_CMP_STRIDE + 1, :]).astype(BF16),
                          w1_ref[l + NSA_CMP_STRIDE], preferred_element_type=F32)
    pre = lo + pltpu.roll(hi, nc - 1, 0)
    act = jax.nn.gelu(pre)
    return jnp.dot(act.astype(BF16), w2_ref[...], preferred_element_type=F32)


def _nsa_compress_kernel(rows_ref, pe_ref, w1_ref, w2_ref, o_ref, *, nc):
    out = _compress_core(lambda l: rows_ref[0, l], pe_ref, w1_ref, w2_ref, nc)
    for g in range(NSA_GROUPS):
        o_ref[0, g] = out[:, g * HEAD_DIM:(g + 1) * HEAD_DIM].astype(o_ref.dtype)


def _block_diag4(w):
    eye = jnp.eye(NSA_GROUPS, dtype=w.dtype)
    out = jnp.einsum("gh,...ij->...gihj", eye, w)
    return out.reshape(w.shape[:-2] + (NSA_GROUPS * HEAD_DIM, NSA_GROUPS * HEAD_DIM))


def nsa_compress_weights(pe, w1, w2):
    pe4 = jnp.tile(pe, (1, NSA_GROUPS))
    w1bd = _block_diag4(w1.reshape(NSA_CMP_LEN, HEAD_DIM, HEAD_DIM)).astype(BF16)
    w2bd = _block_diag4(w2).astype(BF16)
    return pe4, w1bd, w2bd


def nsa_compress_prompt(rows, cw):
    b, s, d = rows.shape
    nc = s // NSA_CMP_STRIDE
    pe4, w1bd, w2bd = cw
    return pl.pallas_call(
        functools.partial(_nsa_compress_kernel, nc=nc),
        grid=(b,),
        in_specs=[pl.BlockSpec((1, NSA_CMP_STRIDE, nc, d), lambda bi: (bi, 0, 0, 0)),
                  pl.BlockSpec(pe4.shape, lambda bi: (0, 0)),
                  pl.BlockSpec(w1bd.shape, lambda bi: (0, 0, 0)),
                  pl.BlockSpec(w2bd.shape, lambda bi: (0, 0))],
        out_specs=pl.BlockSpec((1, NSA_GROUPS, nc, HEAD_DIM), lambda bi: (bi, 0, 0, 0)),
        out_shape=jax.ShapeDtypeStruct((b, NSA_GROUPS, nc, HEAD_DIM), BF16),
        compiler_params=_ARB(1), name="nsa_compress",
    )(rows.reshape(b, nc, NSA_CMP_STRIDE, d).transpose(0, 2, 1, 3), pe4, w1bd, w2bd)


def _select_blocks(imp, pos, ns):
    blk = lax.broadcasted_iota(jnp.int32, imp.shape, 1)
    cur = pos // NSA_SEL_LEN
    valid = blk * NSA_SEL_LEN <= pos
    forced = (blk == 0) | ((blk <= cur) & (blk > cur - NSA_N_LOCAL))
    score = jnp.where(forced, jnp.inf, jnp.where(valid, imp, -jnp.inf))
    rank = jnp.zeros(imp.shape, jnp.int32)
    for i in range(ns):
        ci = score[:, i:i + 1]
        ahead = (ci > score) | ((ci == score) & (blk > i))
        rank = rank + ahead.astype(jnp.int32)
    return rank < NSA_TOP_N


def _nsa_cmp_kernel(q_ref, kc_ref, vc_ref, ov_ref, gate_ref, o_ref, sel_ref, *, tq, nc, ns):
    q0 = pl.program_id(2) * tq
    q = q_ref[0]
    kc, vc = kc_ref[0, 0], vc_ref[0, 0]
    pos = q0 + lax.broadcasted_iota(jnp.int32, (tq, 1), 0)
    c_last = lax.broadcasted_iota(jnp.int32, (1, nc), 1) * NSA_CMP_STRIDE + (NSA_CMP_LEN - 1)
    valid = c_last <= pos
    gate = gate_ref[0, 0]
    psum = jnp.zeros((tq, nc), F32)
    outs = []
    for r in range(NSA_REP):
        qr = q[:, r * HEAD_DIM:(r + 1) * HEAD_DIM]
        s = lax.dot_general(qr, kc, (((1,), (1,)), ((), ())), preferred_element_type=F32)
        s = jnp.where(valid, s, NEG)
        e = jnp.where(valid, jnp.exp(s - jnp.max(s, axis=-1, keepdims=True)), 0.0)
        p = e / jnp.maximum(jnp.sum(e, axis=-1, keepdims=True), 1e-30)
        outs.append(jnp.dot(p.astype(BF16), vc, preferred_element_type=F32) * gate[:, r:r + 1])
        psum = psum + p
    o_ref[0] = jnp.concatenate(outs, axis=-1)
    imp = jnp.dot(psum, ov_ref[...], preferred_element_type=F32, precision=lax.Precision.HIGHEST)
    sel = _select_blocks(imp, pos, ns)
    sel_ref[0, 0] = jnp.where(sel, 1.0, 0.0).astype(sel_ref.dtype)


def nsa_overlap(nc, ns):
    c0 = np.arange(nc)[:, None] * NSA_CMP_STRIDE
    s0 = np.arange(ns)[None, :] * NSA_SEL_LEN
    ov = np.minimum(c0 + NSA_CMP_LEN, s0 + NSA_SEL_LEN) - np.maximum(c0, s0)
    return (np.clip(ov, 0, None) / NSA_CMP_STRIDE).astype(np.float32)


def nsa_cmp_prompt(q, kc, vc, gate):
    b, s, d = q.shape
    tq = min(ATT_TQ, s)
    nc, ns = s // NSA_CMP_STRIDE, s // NSA_SEL_LEN
    gw = NSA_REP * HEAD_DIM
    ov = jnp.asarray(nsa_overlap(nc, ns))
    return pl.pallas_call(
        functools.partial(_nsa_cmp_kernel, tq=tq, nc=nc, ns=ns),
        grid=(b, NSA_GROUPS, s // tq),
        in_specs=[pl.BlockSpec((1, tq, gw), lambda bi, g, qi: (bi, qi, g)),
                  pl.BlockSpec((1, 1, nc, HEAD_DIM), lambda bi, g, qi: (bi, g, 0, 0)),
                  pl.BlockSpec((1, 1, nc, HEAD_DIM), lambda bi, g, qi: (bi, g, 0, 0)),
                  pl.BlockSpec((nc, ns), lambda bi, g, qi: (0, 0)),
                  pl.BlockSpec((1, 1, tq, NSA_REP), lambda bi, g, qi: (bi, g, qi, 0))],
        out_specs=[pl.BlockSpec((1, tq, gw), lambda bi, g, qi: (bi, qi, g)),
                   pl.BlockSpec((1, 1, tq, ns), lambda bi, g, qi: (bi, g, qi, 0))],
        out_shape=[jax.ShapeDtypeStruct((b, s, d), F32),
                   jax.ShapeDtypeStruct((b, NSA_GROUPS, s, ns), BF16)],
        compiler_params=_ARB(3), name="nsa_cmp_attention",
    )(q, kc, vc, ov, gate)


def _nsa_branch_kernel(*refs, tq, tk, mode):
    if mode == "sel":
        q_ref, k_ref, v_ref, gate_ref, sel_ref, e_ref, o_ref = refs
    else:
        q_ref, k_ref, v_ref, gate_ref, o_ref = refs
    q0 = pl.program_id(2) * tq
    q = q_ref[0]
    gate = gate_ref[0, 0]
    lane = lax.broadcasted_iota(jnp.int32, (tq, LANES), 1)
    rows = NSA_REP * tq

    def load_kv(j):
        st = pl.multiple_of(j * tk, tk)
        return k_ref[0, pl.ds(st, tk), :], v_ref[0, pl.ds(st, tk), :]

    group_out = []
    for i in range(2):
        keep = (lane >= 64 * i) & (lane < 64 * (i + 1))
        parts = []
        for r in range(NSA_REP):
            c = 2 * i + r // 2
            x = q[:, c * LANES:(c + 1) * LANES]
            if r % 2 != i:
                x = _swap_halves(x)
            parts.append(jnp.where(keep, x, jnp.zeros_like(x)))
        qs = jnp.concatenate(parts, axis=0)
        if mode == "sel":
            sm = sel_ref[0, i]
            sm4 = jnp.concatenate([sm] * NSA_REP, axis=0)

            def mask_fn(j, diag, sm4=sm4):
                hit = jnp.dot(sm4, e_ref[j], preferred_element_type=F32) > 0.5
                if diag:
                    qpos, kpos = _causal_mask(q0, tq, rows, j, tk)
                    hit = hit & (kpos <= qpos)
                return hit
            j_lo = 0
        else:
            def mask_fn(j, diag):
                qpos, kpos = _causal_mask(q0, tq, rows, j, tk)
                return (kpos <= qpos) & (qpos - kpos < NSA_WINDOW)
            j_lo = jnp.maximum(q0 - (NSA_WINDOW - 1), 0) // tk
        group_out.append(_flash_chunks(qs, load_kv, j_lo, q0 // tk, None, mask_fn, robust=True))
    for c in range(4):
        i = c // 2
        halves = []
        for hh in range(2):
            r = 2 * (c % 2) + hh
            x = group_out[i][r * tq:(r + 1) * tq]
            if hh != i:
                x = pltpu.roll(x, 64, 1)
            halves.append(x * gate[:, 2 * c + hh:2 * c + hh + 1])
        o_ref[0, :, c * LANES:(c + 1) * LANES] = jnp.where(lane < 64, halves[0], halves[1])


SEL_TK = 512
WIN_TK = 256


def nsa_branch_prompt(q, k, v, gate, sel=None):
    b, s, d = q.shape
    mode = "sel" if sel is not None else "win"
    tq = min(ATT_TQ, s)
    tk = min(SEL_TK if mode == "sel" else WIN_TK, s)
    ns = s // NSA_SEL_LEN
    qw = 2 * NSA_REP * HEAD_DIM
    args = [q, k, v, gate]
    in_specs = [pl.BlockSpec((1, tq, qw), lambda bi, gp, qi: (bi, qi, gp)),
                pl.BlockSpec((1, s, LANES), lambda bi, gp, qi: (bi, 0, gp)),
                pl.BlockSpec((1, s, LANES), lambda bi, gp, qi: (bi, 0, gp)),
                pl.BlockSpec((1, 1, tq, 8), lambda bi, gp, qi: (bi, gp, qi, 0))]
    if mode == "sel":
        kb = (np.arange(s) // NSA_SEL_LEN).reshape(s // tk, 1, tk)
        expand = jnp.asarray(kb == np.arange(ns).reshape(1, ns, 1), dtype=BF16)
        args += [sel, expand]
        in_specs += [pl.BlockSpec((1, 2, tq, ns), lambda bi, gp, qi: (bi, gp, qi, 0)),
                     pl.BlockSpec(expand.shape, lambda bi, gp, qi: (0, 0, 0))]
    return pl.pallas_call(
        functools.partial(_nsa_branch_kernel, tq=tq, tk=tk, mode=mode),
        grid=(b, 2, s // tq), in_specs=in_specs,
        out_specs=pl.BlockSpec((1, tq, qw), lambda bi, gp, qi: (bi, qi, gp)),
        out_shape=jax.ShapeDtypeStruct((b, s, d), F32),
        compiler_params=_ARB(3), name="nsa_%s_attention" % mode,
    )(*args)


PAGES_PER_STEP = 8
DEC_HEADS = 16


def _decode_attn_kernel(*refs, n_blk, blk_rows, has_bias, has_self, paged):
    it = iter(refs)
    if paged:
        next(it)
    qt_ref = next(it)
    bias_ref = next(it) if has_bias else None
    if has_self:
        ks_ref, vs_ref = next(it), next(it)
    k_refs = [next(it) for _ in range(n_blk)]
    v_refs = [next(it) for _ in range(n_blk)]
    o_ref = next(it)
    m_sc, l_sc, acc_sc = next(it), next(it), next(it)
    j = pl.program_id(1)

    @pl.when(j == 0)
    def _():
        m_sc[...] = jnp.full(m_sc.shape, NEG, F32)
        l_sc[...] = jnp.zeros(l_sc.shape, F32)
        acc_sc[...] = jnp.zeros(acc_sc.shape, F32)

    qt = qt_ref[0]
    m, l, acc = m_sc[...], l_sc[...], acc_sc[...]
    for i in range(n_blk):
        k = k_refs[i][...].astype(BF16)
        v = v_refs[i][...].astype(BF16)
        s = lax.dot_general(qt, k, (((1,), (1,)), ((), ())), preferred_element_type=F32)
        if has_bias:
            s = s + bias_ref[0, :, i * blk_rows:(i + 1) * blk_rows]
        m_new = jnp.maximum(m, jnp.max(s, axis=-1, keepdims=True))
        alpha = jnp.exp(m - m_new)
        p = jnp.exp(s - m_new)
        l = alpha * l + jnp.sum(p, axis=-1, keepdims=True)
        acc = alpha * acc + jnp.dot(p.astype(BF16), v, preferred_element_type=F32)
        m = m_new
    m_sc[...], l_sc[...], acc_sc[...] = m, l, acc

    @pl.when(j == pl.num_programs(1) - 1)
    def _():
        m, l, acc = m_sc[...], l_sc[...], acc_sc[...]
        if has_self:
            s = jnp.sum(qt.astype(F32) * ks_ref[0], axis=-1, keepdims=True)
            m_new = jnp.maximum(m, s)
            alpha = jnp.exp(m - m_new)
            p = jnp.exp(s - m_new)
            l = alpha * l + p
            acc = alpha * acc + p * vs_ref[0]
        o_ref[0] = acc / jnp.maximum(l, 1e-30)


def decode_attention(qt, k, v, bias=None, k_self=None, v_self=None, page_table=None):
    b, nh, lk = qt.shape
    lv = v.shape[-1]
    paged = page_table is not None
    has_bias, has_self = bias is not None, k_self is not None
    if paged:
        blk_rows, n_blk = k.shape[1], PAGES_PER_STEP
        steps = page_table.shape[1] // n_blk
    else:
        blk_rows, n_blk = min(k.shape[1], 512), 1
        steps = k.shape[1] // blk_rows
    pre = (lambda f: (lambda bi, j, pt: f(bi, j, pt))) if paged else (lambda f: (lambda bi, j: f(bi, j, None)))
    args = [qt]
    in_specs = [pl.BlockSpec((1, nh, lk), pre(lambda bi, j, pt: (bi, 0, 0)))]
    if has_bias:
        args.append(bias)
        in_specs.append(pl.BlockSpec((1, nh, n_blk * blk_rows), pre(lambda bi, j, pt: (bi, 0, j))))
    if has_self:
        args += [k_self, v_self]
        in_specs += [pl.BlockSpec((1, 1, lk), pre(lambda bi, j, pt: (bi, 0, 0))),
                     pl.BlockSpec((1, 1, lv), pre(lambda bi, j, pt: (bi, 0, 0)))]
    for arr, width in ((k, lk), (v, lv)):
        for i in range(n_blk):
            args.append(arr)
            if paged:
                in_specs.append(pl.BlockSpec((None, blk_rows, width),
                                             lambda bi, j, pt, i=i: (pt[bi, j * n_blk + i], 0, 0)))
            else:
                in_specs.append(pl.BlockSpec((None, blk_rows, width), lambda bi, j: (bi, j, 0)))
    out_spec = pl.BlockSpec((1, nh, lv), pre(lambda bi, j, pt: (bi, 0, 0)))
    scratch = [pltpu.VMEM((nh, 1), F32), pltpu.VMEM((nh, 1), F32), pltpu.VMEM((nh, lv), F32)]
    kern = functools.partial(_decode_attn_kernel, n_blk=n_blk, blk_rows=blk_rows,
                             has_bias=has_bias, has_self=has_self, paged=paged)
    out_shape = jax.ShapeDtypeStruct((b, nh, lv), F32)
    if paged:
        return pl.pallas_call(
            kern, out_shape=out_shape,
            grid_spec=pltpu.PrefetchScalarGridSpec(
                num_scalar_prefetch=1, grid=(b, steps), in_specs=in_specs, out_specs=out_spec,
                scratch_shapes=scratch),
            compiler_params=_ARB(2), name="decode_attention_paged",
        )(page_table, *args)
    return pl.pallas_call(
        kern, out_shape=out_shape, grid=(b, steps), in_specs=in_specs, out_specs=out_spec,
        scratch_shapes=scratch, compiler_params=_ARB(2), name="decode_attention",
    )(*args)


def _spread_heads(q, width):
    b = q.shape[0]
    n_kv = width // HEAD_DIM
    owner = (np.arange(DEC_HEADS) * n_kv) // DEC_HEADS
    onehot = jnp.asarray(owner[:, None] == np.arange(n_kv)[None, :], dtype=F32)
    out = q[:, :, None, :] * onehot[None, :, :, None]
    return (out * SCALE).reshape(b, DEC_HEADS, width).astype(BF16)


def _own_lanes(o, width_per_head):
    b, nh, lv = o.shape
    n_kv = lv // width_per_head
    owner = (np.arange(nh) * n_kv) // nh
    o = o.reshape(b, nh, n_kv, width_per_head)
    return jnp.take_along_axis(o, jnp.asarray(owner).reshape(1, nh, 1, 1), axis=2)[:, :, 0]


def _diff_finish_kernel(o0_ref, o1_ref, lam_ref, subln_ref, o_ref, *, lam_init):
    o = o0_ref[...] - lam_ref[...] * o1_ref[...]
    o_ref[...] = _rms(o, subln_ref[...]) * (1.0 - lam_init)


def diff_finish(o0, o1, lam, lam_init, subln):
    m, w = o0.shape
    lam_vec = jnp.broadcast_to(lam.astype(F32).reshape(1, 1), (1, w))
    full = lambda shape: pl.BlockSpec(shape, lambda i: (0, 0))
    return pl.pallas_call(
        functools.partial(_diff_finish_kernel, lam_init=lam_init),
        grid=(1,), in_specs=[full((m, w)), full((m, w)), full((1, w)), full((1, w))],
        out_specs=full((m, w)), out_shape=jax.ShapeDtypeStruct((m, w), F32),
        compiler_params=_ARB(1), name="diff_finish",
    )(o0, o1, lam_vec, subln.reshape(1, w))


def _nsa_cmp_decode_kernel(*refs, n_blk, ncp, ns, past):
    it = iter(refs)
    next(it)
    qt_ref = next(it)
    pages = [[[next(it) for _ in range(n_blk)] for _ in range(2)] for _ in range(2)]
    pek_ref, w1k_ref, w2k_ref, pev_ref, w1v_ref, w2v_ref, ov_ref = (next(it) for _ in range(7))
    o_ref, sel_ref = next(it), next(it)
    ph_refs = [next(it), next(it)]
    j = pl.program_id(1)
    for t in range(2):
        for half in range(2):
            for i in range(n_blk):
                c0 = pl.multiple_of((j * n_blk + i) * 8, 8)
                for l in range(NSA_CMP_STRIDE):
                    ph_refs[t][l, pl.ds(c0, 8), half * LANES:(half + 1) * LANES] = \
                        pages[t][half][i][pl.ds(l, 8, stride=NSA_CMP_STRIDE), :]

    @pl.when(j == pl.num_programs(1) - 1)
    def _():
        kc = _compress_core(lambda l: ph_refs[0][l], pek_ref, w1k_ref, w2k_ref, ncp)
        vc = _compress_core(lambda l: ph_refs[1][l], pev_ref, w1v_ref, w2v_ref, ncp)
        qt = qt_ref[0]
        s = lax.dot_general(qt, kc.astype(BF16), (((1,), (1,)), ((), ())), preferred_element_type=F32)
        valid = lax.broadcasted_iota(jnp.int32, s.shape, 1) < ncp - 1
        s = jnp.where(valid, s, NEG)
        e = jnp.where(valid, jnp.exp(s - jnp.max(s, axis=-1, keepdims=True)), 0.0)
        p = e / jnp.maximum(jnp.sum(e, axis=-1, keepdims=True), 1e-30)
        o_ref[0] = jnp.dot(p.astype(BF16), vc.astype(BF16), preferred_element_type=F32)
        grp = (lax.broadcasted_iota(jnp.int32, (8, DEC_HEADS), 1) // NSA_REP
               == lax.broadcasted_iota(jnp.int32, (8, DEC_HEADS), 0)).astype(F32)
        psum = jnp.dot(grp, p, preferred_element_type=F32, precision=lax.Precision.HIGHEST)
        imp = jnp.dot(psum, ov_ref[...], preferred_element_type=F32, precision=lax.Precision.HIGHEST)
        sel = _select_blocks(imp, jnp.full((8, 1), past, jnp.int32), ns)
        sel_ref[0] = jnp.where(sel, 1.0, 0.0)


def nsa_cmp_decode(qt, pool_k, pool_v, page_table, cw_k, cw_v):
    b = qt.shape[0]
    n_pages = page_table.shape[1]
    past = n_pages * pool_k.shape[1]
    n_blk = PAGES_PER_STEP
    ncp = past // NSA_CMP_STRIDE
    ns = past // NSA_SEL_LEN + 1
    nsp = -(-ns // LANES) * LANES
    ov = jnp.asarray(np.pad(nsa_overlap(ncp, ns), ((0, 0), (0, nsp - ns))))
    args = [qt]
    in_specs = [pl.BlockSpec((1, DEC_HEADS, 4 * HEAD_DIM), lambda bi, j, pt: (bi, 0, 0))]
    for pool in (pool_k, pool_v):
        for half in range(2):
            for i in range(n_blk):
                args.append(pool)
                in_specs.append(pl.BlockSpec((None, pool.shape[1], LANES),
                                             lambda bi, j, pt, i=i, half=half: (pt[bi, j * n_blk + i], 0, half)))
    for a in tuple(cw_k) + tuple(cw_v) + (ov,):
        args.append(a)
        in_specs.append(pl.BlockSpec(a.shape, lambda bi, j, pt, nd=a.ndim: (0,) * nd))
    return pl.pallas_call(
        functools.partial(_nsa_cmp_decode_kernel, n_blk=n_blk, ncp=ncp, ns=ns, past=past),
        out_shape=[jax.ShapeDtypeStruct((b, DEC_HEADS, 4 * HEAD_DIM), F32),
                   jax.ShapeDtypeStruct((b, 8, nsp), F32)],
        grid_spec=pltpu.PrefetchScalarGridSpec(
            num_scalar_prefetch=1, grid=(b, n_pages // n_blk), in_specs=in_specs,
            out_specs=[pl.BlockSpec((1, DEC_HEADS, 4 * HEAD_DIM), lambda bi, j, pt: (bi, 0, 0)),
                       pl.BlockSpec((1, 8, nsp), lambda bi, j, pt: (bi, 0, 0))],
            scratch_shapes=[pltpu.VMEM((NSA_CMP_STRIDE, ncp, 4 * HEAD_DIM), F32)] * 2),
        compiler_params=_ARB(2), name="nsa_cmp_decode",
    )(page_table, *args)


def gather_pages(pool, page_table):
    rows = pool[page_table]
    return rows.reshape((page_table.shape[0], page_table.shape[1] * pool.shape[1]) + pool.shape[2:])


def _pad_cols(w, n):
    return jnp.pad(w, ((0, 0), (0, n - w.shape[1])))


def _pad_vec(v, n):
    return jnp.pad(v, (0, n - v.shape[0]))


def fox_mixer(hp, hs, g, cache_k, cache_v, cache_logf, page_table, w_in, b_f, w_o):
    bp, sp, d = hp.shape
    bs, ds, _ = hs.shape
    past = page_table.shape[1] * cache_k.shape[1]
    hd = FOX_HEADS * HEAD_DIM
    wb = w_in.astype(BF16)
    ws = [wb[:, :hd], wb[:, hd:2 * hd], wb[:, 2 * hd:3 * hd], _pad_cols(wb[:, 3 * hd:], LANES)]
    biases = [None, None, None, _pad_vec(b_f, LANES)]
    wo = w_o.astype(BF16)
    hp2 = hp.reshape(bp * sp, d)
    hs2 = hs.reshape(bs * ds, d)
    cfgs = [ProjCfg(f32=False, bf16=True, scale=SCALE), ProjCfg(bf16=True), ProjCfg(bf16=True),
            ProjCfg(act="logsig")]
    q16, k, k16, v, v16, lf = norm_proj(hp2, g, ws, cfgs, biases)
    logf = lf[:, :FOX_HEADS].reshape(bp, sp, FOX_HEADS)
    c = jnp.cumsum(logf, axis=1)
    o_p = fox_prompt_attention(q16.reshape(bp, sp, hd), k16.reshape(bp, sp, hd), v16.reshape(bp, sp, hd), c)
    hp_new = out_proj([o_p.reshape(bp * sp, hd)], wo, hp2).reshape(bp, sp, d)
    sh = (bp, sp, FOX_HEADS, HEAD_DIM)
    k, v = k.reshape(sh), v.reshape(sh)
    cfgs = [ProjCfg(), ProjCfg(), ProjCfg(), ProjCfg(act="logsig")]
    qs, ks, vs, lfs = norm_proj(hs2, g, ws, cfgs, biases)
    logfs = lfs[:, :FOX_HEADS].reshape(bs, ds, FOX_HEADS)
    c_all = jnp.cumsum(jnp.concatenate([gather_pages(cache_logf, page_table).astype(F32), logfs], axis=1), axis=1)
    bias = (c_all[:, past:] - c_all[:, :past]).transpose(0, 2, 1)
    pool = lambda a: a.reshape(a.shape[0], a.shape[1], hd)
    o_full = decode_attention(_spread_heads(qs.reshape(bs, FOX_HEADS, HEAD_DIM), hd), pool(cache_k), pool(cache_v),
                              bias, ks.reshape(bs, 1, hd), vs.reshape(bs, 1, hd), page_table)
    o_s = _own_lanes(o_full, HEAD_DIM)
    hs_new = out_proj([o_s.reshape(bs * ds, D_MODEL)], wo, hs2).reshape(bs, ds, d)
    shs = (bs, ds, FOX_HEADS, HEAD_DIM)
    ks, vs = ks.reshape(shs), vs.reshape(shs)
    return hp_new, hs_new, (k, v, logf, ks, vs, logfs)


def mlp_both(hp, hs, g, wu, wd):
    wu = wu.astype(BF16)
    wd = wd.astype(BF16)
    yp = mlp(hp.reshape(-1, D_MODEL), g, wu, wd).reshape(hp.shape)
    ys = mlp(hs.reshape(-1, D_MODEL), g, wu, wd).reshape(hs.shape)
    return yp, ys


def nsa_mixer(hp, hs, gn, cache_cmp_k, cache_cmp_v, cache_sel_k, cache_sel_v, state_win_k, state_win_v,
              page_table, w_in, b_gate, pe_k, w1_k, w2_k, pe_v, w1_v, w2_v, w_o):
    bp, sp, d = hp.shape
    bs, ds, _ = hs.shape
    past = page_table.shape[1] * cache_cmp_k.shape[1]
    wb_ = state_win_k.shape[1]
    hq = NSA_HEADS * HEAD_DIM
    kvd = NSA_GROUPS * HEAD_DIM
    wbf = w_in.astype(BF16)
    ws = [wbf[:, :hq]] + [wbf[:, hq + i * kvd:hq + (i + 1) * kvd] for i in range(6)] \
        + [_pad_cols(wbf[:, hq + 6 * kvd:], LANES)]
    biases = [None] * 7 + [_pad_vec(b_gate, LANES)]
    wo = w_o.astype(BF16)
    hp2 = hp.reshape(bp * sp, d)
    hs2 = hs.reshape(bs * ds, d)
    cfgs = [ProjCfg(rope=True, f32=False, bf16=True, scale=SCALE),
            ProjCfg(rope=True), ProjCfg(),
            ProjCfg(rope=True, bf16=True), ProjCfg(bf16=True),
            ProjCfg(rope=True, bf16=True), ProjCfg(bf16=True),
            ProjCfg(act="sigmoid")]
    tabs = rope_tables(jnp.arange(sp, dtype=jnp.int32))
    (q16, kc, vc, ksl, ksl16, vsl, vsl16, kw, kw16, vw, vw16, gt) = norm_proj(hp2, gn, ws, cfgs, biases, tabs)
    r3 = lambda a: a.reshape(bp, sp, a.shape[-1])
    q16 = r3(q16)
    gates = gt[:, :3 * NSA_HEADS].reshape(bp, sp, 3, NSA_HEADS)
    g_cmp = gates[:, :, 0].reshape(bp, sp, NSA_GROUPS, NSA_REP).transpose(0, 2, 1, 3)
    g_sel = gates[:, :, 1].reshape(bp, sp, 2, 8).transpose(0, 2, 1, 3)
    g_win = gates[:, :, 2].reshape(bp, sp, 2, 8).transpose(0, 2, 1, 3)
    kcc = nsa_compress_prompt(r3(kc), nsa_compress_weights(pe_k, w1_k, w2_k))
    vcc = nsa_compress_prompt(r3(vc), nsa_compress_weights(pe_v, w1_v, w2_v))
    o_cmp, sel = nsa_cmp_prompt(q16, kcc, vcc, g_cmp)
    o_sel = nsa_branch_prompt(q16, r3(ksl16), r3(vsl16), g_sel, sel)
    o_win = nsa_branch_prompt(q16, r3(kw16), r3(vw16), g_win)
    hp_new = out_proj([o_cmp.reshape(bp * sp, d), o_sel.reshape(bp * sp, d), o_win.reshape(bp * sp, d)],
                      wo, hp2).reshape(bp, sp, d)
    r4 = lambda a: a.reshape(bp, sp, NSA_GROUPS, HEAD_DIM)
    kc, vc, ksl, vsl, kw, vw = r4(kc), r4(vc), r4(ksl), r4(vsl), r4(kw), r4(vw)
    cfgs = [ProjCfg(rope=True), ProjCfg(rope=True), ProjCfg(), ProjCfg(rope=True), ProjCfg(),
            ProjCfg(rope=True), ProjCfg(), ProjCfg(act="sigmoid")]
    tabs_s = rope_tables(jnp.full((bs * ds,), past, jnp.int32))
    qs, kc_s, vc_s, ksl_s, vsl_s, kw_s, vw_s, gt_s = norm_proj(hs2, gn, ws, cfgs, biases, tabs_s)
    r4s = lambda a: a.reshape(bs, ds, NSA_GROUPS, HEAD_DIM)
    kc_s, vc_s, ksl_s, vsl_s, kw_s, vw_s = r4s(kc_s), r4s(vc_s), r4s(ksl_s), r4s(vsl_s), r4s(kw_s), r4s(vw_s)
    gates_s = gt_s[:, :3 * NSA_HEADS].reshape(bs, ds, 3, NSA_HEADS)
    kvd4 = NSA_GROUPS * HEAD_DIM
    pool = lambda a: a.reshape(a.shape[0], a.shape[1], kvd4)
    row = lambda a: a.reshape(bs, 1, kvd4)
    qt = _spread_heads(qs.reshape(bs, NSA_HEADS, HEAD_DIM), kvd4)
    o_cmp_full, selm = nsa_cmp_decode(qt, pool(cache_cmp_k), pool(cache_cmp_v), page_table,
                                      nsa_compress_weights(pe_k, w1_k, w2_k), nsa_compress_weights(pe_v, w1_v, w2_v))
    n_past_blk = past // NSA_SEL_LEN
    sel_keys = jnp.repeat(selm[:, :NSA_GROUPS, :n_past_blk], NSA_SEL_LEN, axis=2)
    bias_sel = jnp.repeat(jnp.where(sel_keys > 0.5, 0.0, NEG), NSA_REP, axis=1)
    o_sel_full = decode_attention(qt, pool(cache_sel_k), pool(cache_sel_v), bias_sel, row(ksl_s), row(vsl_s), page_table)
    in_win = jnp.arange(wb_) > wb_ - NSA_WINDOW
    bias_win = jnp.broadcast_to(jnp.where(in_win, 0.0, NEG)[None, None, :], (bs, NSA_HEADS, wb_)).astype(F32)
    o_win_full = decode_attention(qt, state_win_k.reshape(bs, wb_, kvd4), state_win_v.reshape(bs, wb_, kvd4),
                                  bias_win, row(kw_s), row(vw_s))
    gs = gates_s.reshape(bs, 3, NSA_HEADS, 1)
    o_s = (gs[:, 0] * _own_lanes(o_cmp_full, HEAD_DIM) + gs[:, 1] * _own_lanes(o_sel_full, HEAD_DIM)
           + gs[:, 2] * _own_lanes(o_win_full, HEAD_DIM))
    hs_new = out_proj([o_s.reshape(bs * ds, d)], wo, hs2).reshape(bs, ds, d)
    kw_all = jnp.concatenate([state_win_k, kw_s], axis=1)
    vw_all = jnp.concatenate([state_win_v, vw_s], axis=1)
    keep = min(NSA_WINDOW, sp)
    return hp_new, hs_new, (kc, vc, ksl, vsl, kw[:, sp - keep:], vw[:, sp - keep:],
                            kc_s, vc_s, ksl_s, vsl_s, kw_all[:, ds:], vw_all[:, ds:])


def diff_mixer(hp, hs, gn, cache_k, cache_v, page_table, w_in, lq1, lk1, lq2, lk2, subln, w_o, layer_idx):
    lam_init = 0.8 - 0.6 * math.exp(-0.3 * layer_idx)
    lam = (jnp.exp(jnp.sum(lq1.astype(F32) * lk1.astype(F32)))
           - jnp.exp(jnp.sum(lq2.astype(F32) * lk2.astype(F32))) + lam_init)
    bp, sp, d = hp.shape
    bs, ds, _ = hs.shape
    past = page_table.shape[1] * cache_k.shape[1]
    wbf = w_in.astype(BF16)
    ws = [wbf[:, :d], wbf[:, d:2 * d], wbf[:, 2 * d:]]
    wo = w_o.astype(BF16)
    hp2 = hp.reshape(bp * sp, d)
    hs2 = hs.reshape(bs * ds, d)
    cfgs = [ProjCfg(rope=True, f32=False, bf16=True, scale=SCALE), ProjCfg(rope=True, bf16=True),
            ProjCfg(bf16=True)]
    tabs = rope_tables(jnp.arange(sp, dtype=jnp.int32))
    q16, k, k16, v, v16 = norm_proj(hp2, gn, ws, cfgs, None, tabs)
    r3 = lambda a: a.reshape(bp, sp, d)
    o_p = diff_prompt_attention(r3(q16), r3(k16), r3(v16), lam, lam_init, subln)
    hp_new = out_proj([o_p.reshape(bp * sp, d)], wo, hp2).reshape(bp, sp, d)
    k = k.reshape(bp, sp, 2 * DIFF_HEADS, HEAD_DIM)
    v = v.reshape(bp, sp, DIFF_HEADS, 2 * HEAD_DIM)
    cfgs = [ProjCfg(rope=True), ProjCfg(rope=True), ProjCfg()]
    tabs_s = rope_tables(jnp.full((bs * ds,), past, jnp.int32))
    qs, ks, vs = norm_proj(hs2, gn, ws, cfgs, None, tabs_s)
    pool = lambda a: a.reshape(a.shape[0], a.shape[1], d)
    o_full = decode_attention(_spread_heads(qs.reshape(bs, 2 * DIFF_HEADS, HEAD_DIM), d), pool(cache_k), pool(cache_v),
                              None, ks.reshape(bs, 1, d), vs.reshape(bs, 1, d), page_table)
    o_sub = _own_lanes(o_full, 2 * HEAD_DIM)
    o_s = diff_finish(o_sub[:, 0::2].reshape(bs * DIFF_HEADS, 2 * HEAD_DIM),
                      o_sub[:, 1::2].reshape(bs * DIFF_HEADS, 2 * HEAD_DIM), lam, lam_init, subln)
    hs_new = out_proj([o_s.reshape(bs * ds, D_MODEL)], wo, hs2).reshape(bs, ds, d)
    ks = ks.reshape(bs, ds, 2 * DIFF_HEADS, HEAD_DIM)
    vs = vs.reshape(bs, ds, DIFF_HEADS, 2 * HEAD_DIM)
    return hp_new, hs_new, (k, v, ks, vs)


def kernel(x_prompt, x_sample, cache_l0_k, cache_l0_v, cache_l0_logf, cache_l1_cmp_k, cache_l1_cmp_v, cache_l1_sel_k, cache_l1_sel_v, state_l1_win_k, state_l1_win_v, cache_l2_k, cache_l2_v, cache_l3_k, cache_l3_v, cache_l3_logf, page_table, l0_norm_mix, l0_fox_w_in, l0_fox_b_f, l0_fox_w_o, l0_norm_mlp, l0_mlp_up, l0_mlp_down, l1_norm_mix, l1_nsa_w_in, l1_nsa_b_gate, l1_nsa_pe_k, l1_nsa_w1_k, l1_nsa_w2_k, l1_nsa_pe_v, l1_nsa_w1_v, l1_nsa_w2_v, l1_nsa_w_o, l1_norm_mlp, l1_mlp_up, l1_mlp_down, l2_norm_mix, l2_diff_w_in, l2_diff_lq1, l2_diff_lk1, l2_diff_lq2, l2_diff_lk2, l2_diff_subln, l2_diff_w_o, l2_norm_mlp, l2_mlp_up, l2_mlp_down, l3_norm_mix, l3_fox_w_in, l3_fox_b_f, l3_fox_w_o, l3_norm_mlp, l3_mlp_up, l3_mlp_down, norm_final):
    hp, hs = x_prompt, x_sample
    state = []
    hp, hs, st = fox_mixer(hp, hs, l0_norm_mix, cache_l0_k, cache_l0_v, cache_l0_logf, page_table,
                           l0_fox_w_in, l0_fox_b_f, l0_fox_w_o)
    state += st
    hp, hs = mlp_both(hp, hs, l0_norm_mlp, l0_mlp_up, l0_mlp_down)
    hp, hs, st = nsa_mixer(hp, hs, l1_norm_mix, cache_l1_cmp_k, cache_l1_cmp_v, cache_l1_sel_k, cache_l1_sel_v,
                           state_l1_win_k, state_l1_win_v, page_table,
                           l1_nsa_w_in, l1_nsa_b_gate, l1_nsa_pe_k, l1_nsa_w1_k, l1_nsa_w2_k,
                           l1_nsa_pe_v, l1_nsa_w1_v, l1_nsa_w2_v, l1_nsa_w_o)
    state += st
    hp, hs = mlp_both(hp, hs, l1_norm_mlp, l1_mlp_up, l1_mlp_down)
    hp, hs, st = diff_mixer(hp, hs, l2_norm_mix, cache_l2_k, cache_l2_v, page_table, l2_diff_w_in,
                            l2_diff_lq1, l2_diff_lk1, l2_diff_lq2, l2_diff_lk2,
                            l2_diff_subln, l2_diff_w_o, DIFF_LAYER)
    state += st
    hp, hs = mlp_both(hp, hs, l2_norm_mlp, l2_mlp_up, l2_mlp_down)
    hp, hs, st = fox_mixer(hp, hs, l3_norm_mix, cache_l3_k, cache_l3_v, cache_l3_logf, page_table,
                           l3_fox_w_in, l3_fox_b_f, l3_fox_w_o)
    state += st
    hp, hs = mlp_both(hp, hs, l3_norm_mlp, l3_mlp_up, l3_mlp_down)
    y_p = final_norm(hp.reshape(-1, D_MODEL), norm_final).reshape(hp.shape)
    y_s = final_norm(hs.reshape(-1, D_MODEL), norm_final).reshape(hs.shape)
    return (y_p, y_s) + tuple(state)
```

```python
import functools
import math
from typing import NamedTuple

import numpy as np
import jax
import jax.numpy as jnp
from jax import lax
from jax.experimental import pallas as pl
from jax.experimental.pallas import tpu as pltpu

F32 = jnp.float32
BF16 = jnp.bfloat16

D_MODEL = 1024
HEAD_DIM = 64
FOX_HEADS = 16
NSA_HEADS = 16
NSA_GROUPS = 4
NSA_REP = 4
NSA_CMP_LEN = 32
NSA_CMP_STRIDE = 16
NSA_SEL_LEN = 64
NSA_TOP_N = 16
NSA_N_LOCAL = 2
NSA_WINDOW = 512
DIFF_HEADS = 8
D_FF = 4096
ROPE_THETA = 10000.0
NORM_EPS = 1e-6
DIFF_LAYER = 2
SCALE = HEAD_DIM ** -0.5

LANES = 128
VMEM_LIMIT = 56 * 1024 * 1024
ROW_TILE = 512
NEG = -1e30

_ARB = lambda n: pltpu.CompilerParams(dimension_semantics=("arbitrary",) * n,
                                      vmem_limit_bytes=VMEM_LIMIT)


def _row_tile(m):
    return ROW_TILE if m % ROW_TILE == 0 else m


def _rms(x, g):
    return x * lax.rsqrt(jnp.mean(x * x, axis=-1, keepdims=True) + NORM_EPS) * g


def _swap_halves(x):
    parts = []
    for c in range(x.shape[-1] // LANES):
        parts += [x[:, c * LANES + 64:(c + 1) * LANES], x[:, c * LANES:c * LANES + 64]]
    return jnp.concatenate(parts, axis=-1)


class ProjCfg(NamedTuple):
    rope: bool = False
    f32: bool = True
    bf16: bool = False
    scale: float = 1.0
    act: str = ""


def _rope(y, cos, sin):
    n = y.shape[-1]
    reps = n // LANES
    if reps > 1:
        cos = jnp.concatenate([cos] * reps, axis=-1)
        sin = jnp.concatenate([sin] * reps, axis=-1)
    lane = lax.broadcasted_iota(jnp.int32, y.shape, 1)
    partner = jnp.where((lane % HEAD_DIM) < HEAD_DIM // 2,
                        pltpu.roll(y, n - HEAD_DIM // 2, 1), pltpu.roll(y, HEAD_DIM // 2, 1))
    return y * cos + partner * sin


def _norm_proj_kernel(*refs, cfgs, use_rope):
    it = iter(refs)
    x_ref, g_ref = next(it), next(it)
    if use_rope:
        cos_ref, sin_ref = next(it), next(it)
    ins = []
    for c in cfgs:
        w_ref = next(it)
        ins.append((w_ref, next(it) if c.act else None))
    xn = _rms(x_ref[...], g_ref[...]).astype(BF16)
    for c, (w_ref, b_ref) in zip(cfgs, ins):
        y = jnp.dot(xn, w_ref[...], preferred_element_type=F32)
        if c.rope:
            y = _rope(y, cos_ref[...], sin_ref[...])
        if c.act:
            z = y + b_ref[...]
            if c.act == "logsig":
                y = jnp.minimum(z, 0.0) - jnp.log1p(jnp.exp(-jnp.abs(z)))
            else:
                y = 1.0 / (1.0 + jnp.exp(-z))
        if c.f32:
            next(it)[...] = y
        if c.bf16:
            next(it)[...] = (y * c.scale).astype(BF16)


def norm_proj(x, g, ws, cfgs, biases=None, rope_tables=None):
    m, d = x.shape
    tm = _row_tile(m)
    use_rope = rope_tables is not None
    args = [x, g.reshape(1, d)]
    in_specs = [pl.BlockSpec((tm, d), lambda i: (i, 0)), pl.BlockSpec((1, d), lambda i: (0, 0))]
    if use_rope:
        nt = rope_tables[0].shape[0] // tm
        for t in rope_tables:
            args.append(t)
            in_specs.append(pl.BlockSpec((tm, LANES), lambda i: (i % nt, 0)))
    out_specs, out_shape = [], []
    for ci, (w, c) in enumerate(zip(ws, cfgs)):
        n = w.shape[1]
        args.append(w)
        in_specs.append(pl.BlockSpec(w.shape, lambda i: (0, 0)))
        if c.act:
            args.append(biases[ci].reshape(1, n))
            in_specs.append(pl.BlockSpec((1, n), lambda i: (0, 0)))
        for flag, dt in ((c.f32, F32), (c.bf16, BF16)):
            if flag:
                out_specs.append(pl.BlockSpec((tm, n), lambda i: (i, 0)))
                out_shape.append(jax.ShapeDtypeStruct((m, n), dt))
    return pl.pallas_call(
        functools.partial(_norm_proj_kernel, cfgs=tuple(cfgs), use_rope=use_rope),
        grid=(m // tm,), in_specs=in_specs, out_specs=out_specs, out_shape=out_shape,
        compiler_params=_ARB(1), name="norm_proj",
    )(*args)


def _out_proj_kernel(*refs, n_in):
    a_refs = refs[:n_in]
    w_ref, res_ref, o_ref = refs[n_in:]
    a = a_refs[0][...]
    for r in a_refs[1:]:
        a = a + r[...]
    o_ref[...] = res_ref[...] + jnp.dot(a.astype(BF16), w_ref[...], preferred_element_type=F32)


def out_proj(a_list, w, res):
    m, d = res.shape
    tm = _row_tile(m)
    k = w.shape[0]
    return pl.pallas_call(
        functools.partial(_out_proj_kernel, n_in=len(a_list)),
        grid=(m // tm,),
        in_specs=[pl.BlockSpec((tm, k), lambda i: (i, 0)) for _ in a_list]
                 + [pl.BlockSpec(w.shape, lambda i: (0, 0)),
                    pl.BlockSpec((tm, d), lambda i: (i, 0))],
        out_specs=pl.BlockSpec((tm, d), lambda i: (i, 0)),
        out_shape=jax.ShapeDtypeStruct((m, d), F32),
        compiler_params=_ARB(1), name="out_proj",
    )(*a_list, w, res)


FF_CHUNK = 1024


def _mlp_kernel(x_ref, g_ref, wu_ref, wd_ref, o_ref):
    x = x_ref[...]
    xn = _rms(x, g_ref[...]).astype(BF16)
    acc = x
    for c in range(D_FF // FF_CHUNK):
        h = jnp.dot(xn, wu_ref[:, c * FF_CHUNK:(c + 1) * FF_CHUNK], preferred_element_type=F32)
        h = jnp.maximum(h, 0.0)
        acc = acc + jnp.dot((h * h).astype(BF16), wd_ref[c * FF_CHUNK:(c + 1) * FF_CHUNK, :],
                            preferred_element_type=F32)
    o_ref[...] = acc


def mlp(x, g, wu, wd):
    m, d = x.shape
    tm = _row_tile(m)
    return pl.pallas_call(
        _mlp_kernel,
        grid=(m // tm,),
        in_specs=[pl.BlockSpec((tm, d), lambda i: (i, 0)),
                  pl.BlockSpec((1, d), lambda i: (0, 0)),
                  pl.BlockSpec(wu.shape, lambda i: (0, 0)),
                  pl.BlockSpec(wd.shape, lambda i: (0, 0))],
        out_specs=pl.BlockSpec((tm, d), lambda i: (i, 0)),
        out_shape=jax.ShapeDtypeStruct((m, d), F32),
        compiler_params=_ARB(1), name="mlp",
    )(x, g.reshape(1, d), wu, wd)


def _final_norm_kernel(x_ref, g_ref, o_ref):
    o_ref[...] = _rms(x_ref[...], g_ref[...])


def final_norm(x, g):
    m, d = x.shape
    tm = _row_tile(m)
    return pl.pallas_call(
        _final_norm_kernel,
        grid=(m // tm,),
        in_specs=[pl.BlockSpec((tm, d), lambda i: (i, 0)),
                  pl.BlockSpec((1, d), lambda i: (0, 0))],
        out_specs=pl.BlockSpec((tm, d), lambda i: (i, 0)),
        out_shape=jax.ShapeDtypeStruct((m, d), F32),
        compiler_params=_ARB(1), name="final_norm",
    )(x, g.reshape(1, d))


def rope_tables(pos):
    half = HEAD_DIM // 2
    inv = ROPE_THETA ** (-jnp.arange(half, dtype=F32) / half)
    ang = pos.astype(F32)[:, None] * inv[None, :]
    cos, sin = jnp.cos(ang), jnp.sin(ang)
    cos = jnp.concatenate([cos, cos, cos, cos], axis=-1)
    sin = jnp.concatenate([-sin, sin, -sin, sin], axis=-1)
    return cos, sin


def _flash_streams(qs, load_kv, j_lo, j_diag, mask_fn, robust, row_bias=None, col_bias_fn=None):
    n = len(qs)
    cols = qs[0].shape[0]

    def step(j, carry, diag):
        k, vt = load_kv(j)
        out = []
        for i in range(n):
            m, l, acc = carry[i]
            s = lax.dot_general(k, qs[i], (((1,), (1,)), ((), ())), preferred_element_type=F32)
            if col_bias_fn is not None:
                s = s + col_bias_fn(i, j)
            msk = mask_fn(i, j, diag) if mask_fn is not None else None
            if msk is not None:
                s = jnp.where(msk, s, NEG)
            rmax = jnp.max(s, axis=0, keepdims=True)
            if row_bias is not None:
                rmax = rmax + row_bias[i]
            m_new = jnp.maximum(m, rmax)
            shift = m_new - row_bias[i] if row_bias is not None else m_new
            alpha = jnp.exp(m - m_new)
            p = jnp.exp(s - shift)
            if msk is not None and robust:
                p = jnp.where(msk, p, 0.0)
            l = alpha * l + jnp.sum(p, axis=0, keepdims=True)
            acc = alpha * acc + jnp.dot(vt, p.astype(BF16), preferred_element_type=F32)
            out.append((m_new, l, acc))
        return tuple(out)

    dv = load_kv(0)[1].shape[0]
    one = (jnp.full((1, cols), NEG, F32), jnp.zeros((1, cols), F32), jnp.zeros((dv, cols), F32))
    carry = lax.fori_loop(j_lo, j_diag, lambda j, c: step(j, c, False), (one,) * n)
    return [acc / jnp.maximum(l, 1e-30) for _, l, acc in step(j_diag, carry, True)]


def _causal_mask(q0, tq, cols, j, tk):
    kpos = j * tk + lax.broadcasted_iota(jnp.int32, (tk, cols), 0)
    qpos = q0 + lax.broadcasted_iota(jnp.int32, (tk, cols), 1) % tq
    return qpos, kpos


def _chunked_t(x, tk):
    b, s, w = x.shape
    return x.reshape(b, s // tk, tk, w // LANES, LANES).transpose(0, 3, 1, 4, 2)


def _head_pair_kernel(*refs, tq, tk, mode, lam_init):
    if mode == "fox":
        q_ref, k_ref, vt_ref, cq_ref, ck_ref, o_ref = refs
    else:
        q_ref, k_ref, vt_ref, lam_ref, subln_ref, o_ref = refs
    q0 = pl.program_id(2) * tq
    q = q_ref[0]
    lane = lax.broadcasted_iota(jnp.int32, (tq, LANES), 1)
    zero = jnp.zeros_like(q)

    def load_kv(j):
        st = pl.multiple_of(j * tk, tk)
        return k_ref[0, pl.ds(st, tk), :], vt_ref[0, 0, j]

    def mask_fn(i, j, diag):
        if not diag:
            return None
        qpos, kpos = _causal_mask(q0, tq, tq, j, tk)
        return kpos <= qpos

    qs = [jnp.where((lane >= 64 * i) & (lane < 64 * (i + 1)), q, zero) for i in range(2)]
    if mode == "fox":
        outs = _flash_streams(qs, load_kv, 0, q0 // tk, mask_fn, False,
                              row_bias=[cq_ref[0, 0, i:i + 1, :] for i in range(2)],
                              col_bias_fn=lambda i, j: -ck_ref[0, 0, j][:, i:i + 1])
        sub = lax.broadcasted_iota(jnp.int32, outs[0].shape, 0)
        ot = jnp.where(sub < 64, outs[0], outs[1])
    else:
        outs = _flash_streams(qs, load_kv, 0, q0 // tk, mask_fn, False)
        ot = outs[0] - lam_ref[...] * outs[1]
        ms = jnp.mean(ot * ot, axis=0, keepdims=True)
        ot = ot * lax.rsqrt(ms + NORM_EPS) * subln_ref[...] * (1.0 - lam_init)
    o_ref[0] = ot.T.astype(o_ref.dtype)


ATT_TQ = 512
ATT_TK = 1024
ATT_TQ_NSA = 256


def fox_prompt_attention(q, k, v, c):
    b, s, d = q.shape
    tq, tk = min(ATT_TQ, s), min(ATT_TK, s)
    hp = d // LANES
    cq = c.reshape(b, s, hp, 2).transpose(0, 2, 3, 1)
    ck = c.reshape(b, s // tk, tk, hp, 2).transpose(0, 3, 1, 2, 4)
    return pl.pallas_call(
        functools.partial(_head_pair_kernel, tq=tq, tk=tk, mode="fox", lam_init=0.0),
        grid=(b, hp, s // tq),
        in_specs=[pl.BlockSpec((1, tq, LANES), lambda bi, h, qi: (bi, qi, h)),
                  pl.BlockSpec((1, s, LANES), lambda bi, h, qi: (bi, 0, h)),
                  pl.BlockSpec((1, 1, s // tk, LANES, tk), lambda bi, h, qi: (bi, h, 0, 0, 0)),
                  pl.BlockSpec((1, 1, 2, tq), lambda bi, h, qi: (bi, h, 0, qi)),
                  pl.BlockSpec((1, 1, s // tk, tk, 2), lambda bi, h, qi: (bi, h, 0, 0, 0))],
        out_specs=pl.BlockSpec((1, tq, LANES), lambda bi, h, qi: (bi, qi, h)),
        out_shape=jax.ShapeDtypeStruct((b, s, d), BF16),
        compiler_params=_ARB(3), name="fox_prompt_attention",
    )(q, k, _chunked_t(v, tk), cq, ck)


def diff_prompt_attention(q, k, v, lam, lam_init, subln):
    b, s, d = q.shape
    tq, tk = min(ATT_TQ, s), min(ATT_TK, s)
    hp = d // LANES
    lam_vec = jnp.broadcast_to(lam.astype(F32).reshape(1, 1), (1, tq))
    return pl.pallas_call(
        functools.partial(_head_pair_kernel, tq=tq, tk=tk, mode="diff", lam_init=lam_init),
        grid=(b, hp, s // tq),
        in_specs=[pl.BlockSpec((1, tq, LANES), lambda bi, h, qi: (bi, qi, h)),
                  pl.BlockSpec((1, s, LANES), lambda bi, h, qi: (bi, 0, h)),
                  pl.BlockSpec((1, 1, s // tk, LANES, tk), lambda bi, h, qi: (bi, h, 0, 0, 0)),
                  pl.BlockSpec((1, tq), lambda bi, h, qi: (0, 0)),
                  pl.BlockSpec((LANES, 1), lambda bi, h, qi: (0, 0))],
        out_specs=pl.BlockSpec((1, tq, LANES), lambda bi, h, qi: (bi, qi, h)),
        out_shape=jax.ShapeDtypeStruct((b, s, d), BF16),
        compiler_params=_ARB(3), name="diff_prompt_attention",
    )(q, k, _chunked_t(v, tk), lam_vec, subln.reshape(LANES, 1))


def _compress_core(get_phase, pe_ref, w1_ref, w2_ref, nc):
    lo = jnp.zeros((nc, 4 * HEAD_DIM), F32)
    hi = jnp.zeros((nc, 4 * HEAD_DIM), F32)
    for l in range(NSA_CMP_STRIDE):
        ph = get_phase(l)
        lo = lo + jnp.dot((ph + pe_ref[l:l + 1, :]).astype(BF16), w1_ref[l],
                          preferred_element_type=F32)
        hi = hi + jnp.dot((ph + pe_ref[l + NSA_CMP_STRIDE:l + NSA_CMP_STRIDE + 1, :]).astype(BF16),
                          w1_ref[l + NSA_CMP_STRIDE], preferred_element_type=F32)
    pre = lo + pltpu.roll(hi, nc - 1, 0)
    act = jax.nn.gelu(pre)
    return jnp.dot(act.astype(BF16), w2_ref[...], preferred_element_type=F32)


def _nsa_compress_kernel(rows_ref, pe_ref, w1_ref, w2_ref, o_ref, *, nc):
    out = _compress_core(lambda l: rows_ref[0, l], pe_ref, w1_ref, w2_ref, nc)
    for g in range(NSA_GROUPS):
        o_ref[0, g] = out[:, g * HEAD_DIM:(g + 1) * HEAD_DIM].astype(o_ref.dtype)


def _block_diag4(w):
    eye = jnp.eye(NSA_GROUPS, dtype=w.dtype)
    out = jnp.einsum("gh,...ij->...gihj", eye, w)
    return out.reshape(w.shape[:-2] + (NSA_GROUPS * HEAD_DIM, NSA_GROUPS * HEAD_DIM))


def nsa_compress_weights(pe, w1, w2):
    pe4 = jnp.tile(pe, (1, NSA_GROUPS))
    w1bd = _block_diag4(w1.reshape(NSA_CMP_LEN, HEAD_DIM, HEAD_DIM)).astype(BF16)
    w2bd = _block_diag4(w2).astype(BF16)
    return pe4, w1bd, w2bd


def nsa_compress_prompt(rows, cw):
    b, s, d = rows.shape
    nc = s // NSA_CMP_STRIDE
    pe4, w1bd, w2bd = cw
    return pl.pallas_call(
        functools.partial(_nsa_compress_kernel, nc=nc),
        grid=(b,),
        in_specs=[pl.BlockSpec((1, NSA_CMP_STRIDE, nc, d), lambda bi: (bi, 0, 0, 0)),
                  pl.BlockSpec(pe4.shape, lambda bi: (0, 0)),
                  pl.BlockSpec(w1bd.shape, lambda bi: (0, 0, 0)),
                  pl.BlockSpec(w2bd.shape, lambda bi: (0, 0))],
        out_specs=pl.BlockSpec((1, NSA_GROUPS, nc, HEAD_DIM), lambda bi: (bi, 0, 0, 0)),
        out_shape=jax.ShapeDtypeStruct((b, NSA_GROUPS, nc, HEAD_DIM), BF16),
        compiler_params=_ARB(1), name="nsa_compress",
    )(rows.reshape(b, nc, NSA_CMP_STRIDE, d).transpose(0, 2, 1, 3), pe4, w1bd, w2bd)


def _select_blocks(imp, pos, ns):
    blk = lax.broadcasted_iota(jnp.int32, imp.shape, 1)
    cur = pos // NSA_SEL_LEN
    valid = blk * NSA_SEL_LEN <= pos
    forced = (blk == 0) | ((blk <= cur) & (blk > cur - NSA_N_LOCAL))
    score = jnp.where(forced, jnp.inf, jnp.where(valid, imp, -jnp.inf))
    rank = jnp.zeros(imp.shape, jnp.int32)
    for i in range(ns):
        ci = score[:, i:i + 1]
        ahead = (ci > score) | ((ci == score) & (blk > i))
        rank = rank + ahead.astype(jnp.int32)
    return rank < NSA_TOP_N


def _select_blocks_t(imp, pos, ns):
    blk = lax.broadcasted_iota(jnp.int32, imp.shape, 0)
    cur = pos // NSA_SEL_LEN
    valid = blk * NSA_SEL_LEN <= pos
    forced = (blk == 0) | ((blk <= cur) & (blk > cur - NSA_N_LOCAL))
    score = jnp.where(forced, jnp.inf, jnp.where(valid, imp, -jnp.inf))
    rank = jnp.zeros(imp.shape, jnp.int32)
    for i in range(ns):
        ci = score[i:i + 1, :]
        ahead = (ci > score) | ((ci == score) & (blk > i))
        rank = rank + ahead.astype(jnp.int32)
    return rank < NSA_TOP_N


def _nsa_cmp_kernel(q_ref, kc_ref, vc_ref, ov_ref, gate_ref, o_ref, sel_ref, *, tq, nc, ns):
    q0 = pl.program_id(2) * tq
    q = q_ref[0]
    kc, vc = kc_ref[0, 0], vc_ref[0, 0]
    pos = q0 + lax.broadcasted_iota(jnp.int32, (tq, 1), 0)
    c_last = lax.broadcasted_iota(jnp.int32, (1, nc), 1) * NSA_CMP_STRIDE + (NSA_CMP_LEN - 1)
    valid = c_last <= pos
    gate = gate_ref[0, 0]
    psum = jnp.zeros((tq, nc), F32)
    outs = []
    for r in range(NSA_REP):
        qr = q[:, r * HEAD_DIM:(r + 1) * HEAD_DIM]
        s = lax.dot_general(qr, kc, (((1,), (1,)), ((), ())), preferred_element_type=F32)
        s = jnp.where(valid, s, NEG)
        e = jnp.where(valid, jnp.exp(s - jnp.max(s, axis=-1, keepdims=True)), 0.0)
        p = e / jnp.maximum(jnp.sum(e, axis=-1, keepdims=True), 1e-30)
        outs.append(jnp.dot(p.astype(BF16), vc, preferred_element_type=F32) * gate[:, r:r + 1])
        psum = psum + p
    o_ref[0] = jnp.concatenate(outs, axis=-1)
    imp = lax.dot_general(ov_ref[...], psum, (((1,), (1,)), ((), ())), preferred_element_type=F32,
                          precision=lax.Precision.HIGHEST)
    pos_row = q0 + lax.broadcasted_iota(jnp.int32, (1, tq), 1)
    sel = _select_blocks_t(imp, pos_row, ns)
    sel_ref[0, 0] = jnp.where(sel, 1.0, 0.0).astype(sel_ref.dtype)


def nsa_overlap(nc, ns):
    c0 = np.arange(nc)[:, None] * NSA_CMP_STRIDE
    s0 = np.arange(ns)[None, :] * NSA_SEL_LEN
    ov = np.minimum(c0 + NSA_CMP_LEN, s0 + NSA_SEL_LEN) - np.maximum(c0, s0)
    return (np.clip(ov, 0, None) / NSA_CMP_STRIDE).astype(np.float32)


def nsa_cmp_prompt(q, kc, vc, gate):
    b, s, d = q.shape
    tq = min(ATT_TQ_NSA, s)
    nc, ns = s // NSA_CMP_STRIDE, s // NSA_SEL_LEN
    gw = NSA_REP * HEAD_DIM
    ov = jnp.asarray(nsa_overlap(nc, ns).T)
    return pl.pallas_call(
        functools.partial(_nsa_cmp_kernel, tq=tq, nc=nc, ns=ns),
        grid=(b, NSA_GROUPS, s // tq),
        in_specs=[pl.BlockSpec((1, tq, gw), lambda bi, g, qi: (bi, qi, g)),
                  pl.BlockSpec((1, 1, nc, HEAD_DIM), lambda bi, g, qi: (bi, g, 0, 0)),
                  pl.BlockSpec((1, 1, nc, HEAD_DIM), lambda bi, g, qi: (bi, g, 0, 0)),
                  pl.BlockSpec((ns, nc), lambda bi, g, qi: (0, 0)),
                  pl.BlockSpec((1, 1, tq, NSA_REP), lambda bi, g, qi: (bi, g, qi, 0))],
        out_specs=[pl.BlockSpec((1, tq, gw), lambda bi, g, qi: (bi, qi, g)),
                   pl.BlockSpec((1, 1, ns, tq), lambda bi, g, qi: (bi, g, 0, qi))],
        out_shape=[jax.ShapeDtypeStruct((b, s, d), F32),
                   jax.ShapeDtypeStruct((b, NSA_GROUPS, ns, s), BF16)],
        compiler_params=_ARB(3), name="nsa_cmp_attention",
    )(q, kc, vc, ov, gate)


def _nsa_branch_kernel(*refs, tq, tk, mode):
    if mode == "sel":
        q_ref, k_ref, vt_ref, gate_ref, sel_ref, e_ref, o_ref = refs
    else:
        q_ref, k_ref, vt_ref, gate_ref, o_ref = refs
    q0 = pl.program_id(2) * tq
    q = q_ref[0]
    gate = gate_ref[0, 0]
    lane = lax.broadcasted_iota(jnp.int32, (tq, LANES), 1)
    cols = NSA_REP * tq

    def load_kv(j):
        st = pl.multiple_of(j * tk, tk)
        return k_ref[0, pl.ds(st, tk), :], vt_ref[0, 0, j]

    qs, sms = [], []
    for i in range(2):
        keep = (lane >= 64 * i) & (lane < 64 * (i + 1))
        parts = []
        for r in range(NSA_REP):
            c = 2 * i + r // 2
            x = q[:, c * LANES:(c + 1) * LANES]
            if r % 2 != i:
                x = _swap_halves(x)
            parts.append(jnp.where(keep, x, jnp.zeros_like(x)))
        qs.append(jnp.concatenate(parts, axis=0))
        if mode == "sel":
            sms.append(jnp.concatenate([sel_ref[0, i]] * NSA_REP, axis=1))
    if mode == "sel":
        def mask_fn(i, j, diag):
            hit = jnp.dot(e_ref[j], sms[i], preferred_element_type=F32) > 0.5
            if diag:
                qpos, kpos = _causal_mask(q0, tq, cols, j, tk)
                hit = hit & (kpos <= qpos)
            return hit
        j_lo = 0
    else:
        def mask_fn(i, j, diag):
            qpos, kpos = _causal_mask(q0, tq, cols, j, tk)
            return (kpos <= qpos) & (qpos - kpos < NSA_WINDOW)
        j_lo = jnp.maximum(q0 - (NSA_WINDOW - 1), 0) // tk
    outs = _flash_streams(qs, load_kv, j_lo, q0 // tk, mask_fn, True)
    pieces = []
    for i in range(2):
        for r in range(NSA_REP):
            hh = NSA_REP * i + r
            pieces.append(outs[i][64 * i:64 * (i + 1), r * tq:(r + 1) * tq] * gate[hh:hh + 1, :])
    o_ref[0] = jnp.concatenate(pieces, axis=0).T


SEL_TK = 512
WIN_TK = 256


def nsa_branch_prompt(q, k, v, gate, sel=None):
    b, s, d = q.shape
    mode = "sel" if sel is not None else "win"
    tq = min(ATT_TQ_NSA, s)
    tk = min(SEL_TK if mode == "sel" else WIN_TK, s)
    ns = s // NSA_SEL_LEN
    qw = 2 * NSA_REP * HEAD_DIM
    args = [q, k, _chunked_t(v, tk), gate]
    in_specs = [pl.BlockSpec((1, tq, qw), lambda bi, gp, qi: (bi, qi, gp)),
                pl.BlockSpec((1, s, LANES), lambda bi, gp, qi: (bi, 0, gp)),
                pl.BlockSpec((1, 1, s // tk, LANES, tk), lambda bi, gp, qi: (bi, gp, 0, 0, 0)),
                pl.BlockSpec((1, 1, 8, tq), lambda bi, gp, qi: (bi, gp, 0, qi))]
    if mode == "sel":
        kb = (np.arange(s) // NSA_SEL_LEN).reshape(s // tk, tk, 1)
        expand = jnp.asarray(kb == np.arange(ns).reshape(1, 1, ns), dtype=BF16)
        args += [sel, expand]
        in_specs += [pl.BlockSpec((1, 2, ns, tq), lambda bi, gp, qi: (bi, gp, 0, qi)),
                     pl.BlockSpec(expand.shape, lambda bi, gp, qi: (0, 0, 0))]
    return pl.pallas_call(
        functools.partial(_nsa_branch_kernel, tq=tq, tk=tk, mode=mode),
        grid=(b, 2, s // tq), in_specs=in_specs,
        out_specs=pl.BlockSpec((1, tq, qw), lambda bi, gp, qi: (bi, qi, gp)),
        out_shape=jax.ShapeDtypeStruct((b, s, d), F32),
        compiler_params=_ARB(3), name="nsa_%s_attention" % mode,
    )(*args)


PAGES_PER_STEP = 8
DEC_HEADS = 16


def _decode_attn_kernel(*refs, n_blk, blk_rows, has_bias, has_self, paged):
    it = iter(refs)
    if paged:
        next(it)
    qt_ref = next(it)
    bias_ref = next(it) if has_bias else None
    if has_self:
        ks_ref, vs_ref = next(it), next(it)
    k_refs = [next(it) for _ in range(n_blk)]
    v_refs = [next(it) for _ in range(n_blk)]
    o_ref = next(it)
    m_sc, l_sc, acc_sc = next(it), next(it), next(it)
    j = pl.program_id(1)

    @pl.when(j == 0)
    def _():
        m_sc[...] = jnp.full(m_sc.shape, NEG, F32)
        l_sc[...] = jnp.zeros(l_sc.shape, F32)
        acc_sc[...] = jnp.zeros(acc_sc.shape, F32)

    qt = qt_ref[0]
    m, l, acc = m_sc[...], l_sc[...], acc_sc[...]
    for i in range(n_blk):
        k = k_refs[i][...].astype(BF16)
        v = v_refs[i][...].astype(BF16)
        s = lax.dot_general(qt, k, (((1,), (1,)), ((), ())), preferred_element_type=F32)
        if has_bias:
            s = s + bias_ref[0, :, i * blk_rows:(i + 1) * blk_rows]
        m_new = jnp.maximum(m, jnp.max(s, axis=-1, keepdims=True))
        alpha = jnp.exp(m - m_new)
        p = jnp.exp(s - m_new)
        l = alpha * l + jnp.sum(p, axis=-1, keepdims=True)
        acc = alpha * acc + jnp.dot(p.astype(BF16), v, preferred_element_type=F32)
        m = m_new
    m_sc[...], l_sc[...], acc_sc[...] = m, l, acc

    @pl.when(j == pl.num_programs(1) - 1)
    def _():
        m, l, acc = m_sc[...], l_sc[...], acc_sc[...]
        if has_self:
            s = jnp.sum(qt.astype(F32) * ks_ref[0], axis=-1, keepdims=True)
            m_new = jnp.maximum(m, s)
            alpha = jnp.exp(m - m_new)
            p = jnp.exp(s - m_new)
            l = alpha * l + p
            acc = alpha * acc + p * vs_ref[0]
        o_ref[0] = acc / jnp.maximum(l, 1e-30)


def decode_attention(qt, k, v, bias=None, k_self=None, v_self=None, page_table=None):
    b, nh, lk = qt.shape
    lv = v.shape[-1]
    paged = page_table is not None
    has_bias, has_self = bias is not None, k_self is not None
    if paged:
        blk_rows, n_blk = k.shape[1], PAGES_PER_STEP
        steps = page_table.shape[1] // n_blk
    else:
        blk_rows, n_blk = min(k.shape[1], 512), 1
        steps = k.shape[1] // blk_rows
    pre = (lambda f: (lambda bi, j, pt: f(bi, j, pt))) if paged else (lambda f: (lambda bi, j: f(bi, j, None)))
    args = [qt]
    in_specs = [pl.BlockSpec((1, nh, lk), pre(lambda bi, j, pt: (bi, 0, 0)))]
    if has_bias:
        args.append(bias)
        in_specs.append(pl.BlockSpec((1, nh, n_blk * blk_rows), pre(lambda bi, j, pt: (bi, 0, j))))
    if has_self:
        args += [k_self, v_self]
        in_specs += [pl.BlockSpec((1, 1, lk), pre(lambda bi, j, pt: (bi, 0, 0))),
                     pl.BlockSpec((1, 1, lv), pre(lambda bi, j, pt: (bi, 0, 0)))]
    for arr, width in ((k, lk), (v, lv)):
        for i in range(n_blk):
            args.append(arr)
            if paged:
                in_specs.append(pl.BlockSpec((None, blk_rows, width),
                                             lambda bi, j, pt, i=i: (pt[bi, j * n_blk + i], 0, 0)))
            else:
                in_specs.append(pl.BlockSpec((None, blk_rows, width), lambda bi, j: (bi, j, 0)))
    out_spec = pl.BlockSpec((1, nh, lv), pre(lambda bi, j, pt: (bi, 0, 0)))
    scratch = [pltpu.VMEM((nh, 1), F32), pltpu.VMEM((nh, 1), F32), pltpu.VMEM((nh, lv), F32)]
    kern = functools.partial(_decode_attn_kernel, n_blk=n_blk, blk_rows=blk_rows,
                             has_bias=has_bias, has_self=has_self, paged=paged)
    out_shape = jax.ShapeDtypeStruct((b, nh, lv), F32)
    if paged:
        return pl.pallas_call(
            kern, out_shape=out_shape,
            grid_spec=pltpu.PrefetchScalarGridSpec(
                num_scalar_prefetch=1, grid=(b, steps), in_specs=in_specs, out_specs=out_spec,
                scratch_shapes=scratch),
            compiler_params=_ARB(2), name="decode_attention_paged",
        )(page_table, *args)
    return pl.pallas_call(
        kern, out_shape=out_shape, grid=(b, steps), in_specs=in_specs, out_specs=out_spec,
        scratch_shapes=scratch, compiler_params=_ARB(2), name="decode_attention",
    )(*args)


def _decode_keylane_kernel(pt_ref, q_ref, bias_ref, ks_ref, vs_ref, *rest, n_blk):
    k_refs, v_refs = rest[:n_blk], rest[n_blk:2 * n_blk]
    o_ref, m_sc, l_sc, acc_sc = rest[2 * n_blk:]
    j = pl.program_id(1)

    @pl.when(j == 0)
    def _():
        m_sc[...] = jnp.full(m_sc.shape, NEG, F32)
        l_sc[...] = jnp.zeros(l_sc.shape, F32)
        acc_sc[...] = jnp.zeros(acc_sc.shape, F32)

    q = q_ref[0]
    qb = jnp.broadcast_to(q, acc_sc.shape)
    m, l, acc = m_sc[...], l_sc[...], acc_sc[...]
    for i in range(n_blk):
        kt = k_refs[i][...][:, None]
        vt = v_refs[i][...][:, None]
        s = jnp.sum(qb * kt, axis=2, keepdims=True) + bias_ref[0, i]
        m_new = jnp.maximum(m, jnp.max(s, axis=-1, keepdims=True))
        alpha = jnp.exp(m - m_new)
        p = jnp.exp(s - m_new)
        l = alpha * l + jnp.sum(p, axis=-1, keepdims=True)
        acc = alpha * acc + p * vt
        m = m_new
    m_sc[...], l_sc[...], acc_sc[...] = m, l, acc

    @pl.when(j == pl.num_programs(1) - 1)
    def _():
        m, l, acc = m_sc[...], l_sc[...], acc_sc[...]
        s = jnp.sum(q * ks_ref[0], axis=2, keepdims=True)
        m_new = jnp.maximum(m, s)
        alpha = jnp.exp(m - m_new)
        p = jnp.exp(s - m_new)
        l = alpha * l + p
        o = alpha * jnp.sum(acc, axis=-1, keepdims=True) + p * vs_ref[0]
        o_ref[0] = o / jnp.maximum(l, 1e-30)


def decode_attention_keylane(q, k_pool, v_pool, bias, k_self, v_self, page_table):
    b, n_kv, rep, dh = q.shape
    n_pages = page_table.shape[1]
    psz = k_pool.shape[-1]
    n_blk = PAGES_PER_STEP
    bias6 = bias.reshape(b, n_kv, n_pages, 1, 1, psz).transpose(0, 2, 1, 3, 4, 5)
    args = [q.reshape(b, n_kv, rep, dh, 1), bias6, k_self.reshape(b, n_kv, 1, dh, 1), v_self.reshape(b, n_kv, 1, dh, 1)]
    in_specs = [pl.BlockSpec((1, n_kv, rep, dh, 1), lambda bi, j, pt: (bi, 0, 0, 0, 0)),
                pl.BlockSpec((1, n_blk, n_kv, 1, 1, psz), lambda bi, j, pt: (bi, j, 0, 0, 0, 0)),
                pl.BlockSpec((1, n_kv, 1, dh, 1), lambda bi, j, pt: (bi, 0, 0, 0, 0)),
                pl.BlockSpec((1, n_kv, 1, dh, 1), lambda bi, j, pt: (bi, 0, 0, 0, 0))]
    for pool in (k_pool, v_pool):
        for i in range(n_blk):
            args.append(pool)
            in_specs.append(pl.BlockSpec((None, n_kv, dh, psz),
                                         lambda bi, j, pt, i=i: (pt[bi, j * n_blk + i], 0, 0, 0)))
    out = pl.pallas_call(
        functools.partial(_decode_keylane_kernel, n_blk=n_blk),
        out_shape=jax.ShapeDtypeStruct((b, n_kv, rep, dh, 1), F32),
        grid_spec=pltpu.PrefetchScalarGridSpec(
            num_scalar_prefetch=1, grid=(b, n_pages // n_blk), in_specs=in_specs,
            out_specs=pl.BlockSpec((1, n_kv, rep, dh, 1), lambda bi, j, pt: (bi, 0, 0, 0, 0)),
            scratch_shapes=[pltpu.VMEM((n_kv, rep, 1, 1), F32), pltpu.VMEM((n_kv, rep, 1, 1), F32),
                            pltpu.VMEM((n_kv, rep, dh, psz), F32)]),
        compiler_params=_ARB(2), name="decode_attention_keylane",
    )(page_table, *args)
    return out.reshape(b, n_kv, rep, dh)


def _keys_on_lanes(pool):
    return pool.transpose(0, 2, 3, 1)


def _spread_heads(q, width):
    b = q.shape[0]
    n_kv = width // HEAD_DIM
    owner = (np.arange(DEC_HEADS) * n_kv) // DEC_HEADS
    onehot = jnp.asarray(owner[:, None] == np.arange(n_kv)[None, :], dtype=F32)
    out = q[:, :, None, :] * onehot[None, :, :, None]
    return (out * SCALE).reshape(b, DEC_HEADS, width).astype(BF16)


def _own_lanes(o, width_per_head):
    b, nh, lv = o.shape
    n_kv = lv // width_per_head
    owner = (np.arange(nh) * n_kv) // nh
    o = o.reshape(b, nh, n_kv, width_per_head)
    return jnp.take_along_axis(o, jnp.asarray(owner).reshape(1, nh, 1, 1), axis=2)[:, :, 0]


def _diff_finish_kernel(o0_ref, o1_ref, lam_ref, subln_ref, o_ref, *, lam_init):
    o = o0_ref[...] - lam_ref[...] * o1_ref[...]
    o_ref[...] = _rms(o, subln_ref[...]) * (1.0 - lam_init)


def diff_finish(o0, o1, lam, lam_init, subln):
    m, w = o0.shape
    lam_vec = jnp.broadcast_to(lam.astype(F32).reshape(1, 1), (1, w))
    full = lambda shape: pl.BlockSpec(shape, lambda i: (0, 0))
    return pl.pallas_call(
        functools.partial(_diff_finish_kernel, lam_init=lam_init),
        grid=(1,), in_specs=[full((m, w)), full((m, w)), full((1, w)), full((1, w))],
        out_specs=full((m, w)), out_shape=jax.ShapeDtypeStruct((m, w), F32),
        compiler_params=_ARB(1), name="diff_finish",
    )(o0, o1, lam_vec, subln.reshape(1, w))


def _nsa_cmp_decode_kernel(*refs, n_blk, ncp, ns, past):
    it = iter(refs)
    next(it)
    qt_ref = next(it)
    pages = [[[next(it) for _ in range(n_blk)] for _ in range(2)] for _ in range(2)]
    pek_ref, w1k_ref, w2k_ref, pev_ref, w1v_ref, w2v_ref, ov_ref = (next(it) for _ in range(7))
    o_ref, sel_ref = next(it), next(it)
    ph_refs = [next(it), next(it)]
    j = pl.program_id(1)
    for t in range(2):
        for half in range(2):
            for i in range(n_blk):
                c0 = pl.multiple_of((j * n_blk + i) * 8, 8)
                for l in range(NSA_CMP_STRIDE):
                    ph_refs[t][l, pl.ds(c0, 8), half * LANES:(half + 1) * LANES] = \
                        pages[t][half][i][pl.ds(l, 8, stride=NSA_CMP_STRIDE), :]

    @pl.when(j == pl.num_programs(1) - 1)
    def _():
        kc = _compress_core(lambda l: ph_refs[0][l], pek_ref, w1k_ref, w2k_ref, ncp)
        vc = _compress_core(lambda l: ph_refs[1][l], pev_ref, w1v_ref, w2v_ref, ncp)
        qt = qt_ref[0]
        s = lax.dot_general(qt, kc.astype(BF16), (((1,), (1,)), ((), ())), preferred_element_type=F32)
        valid = lax.broadcasted_iota(jnp.int32, s.shape, 1) < ncp - 1
        s = jnp.where(valid, s, NEG)
        e = jnp.where(valid, jnp.exp(s - jnp.max(s, axis=-1, keepdims=True)), 0.0)
        p = e / jnp.maximum(jnp.sum(e, axis=-1, keepdims=True), 1e-30)
        o_ref[0] = jnp.dot(p.astype(BF16), vc.astype(BF16), preferred_element_type=F32)
        grp = (lax.broadcasted_iota(jnp.int32, (8, DEC_HEADS), 1) // NSA_REP
               == lax.broadcasted_iota(jnp.int32, (8, DEC_HEADS), 0)).astype(F32)
        psum = jnp.dot(grp, p, preferred_element_type=F32, precision=lax.Precision.HIGHEST)
        imp = jnp.dot(psum, ov_ref[...], preferred_element_type=F32, precision=lax.Precision.HIGHEST)
        sel = _select_blocks(imp, jnp.full((8, 1), past, jnp.int32), ns)
        sel_ref[0] = jnp.where(sel, 1.0, 0.0)


def nsa_cmp_decode(qt, pool_k, pool_v, page_table, cw_k, cw_v):
    b = qt.shape[0]
    n_pages = page_table.shape[1]
    past = n_pages * pool_k.shape[1]
    n_blk = PAGES_PER_STEP
    ncp = past // NSA_CMP_STRIDE
    ns = past // NSA_SEL_LEN + 1
    nsp = -(-ns // LANES) * LANES
    ov = jnp.asarray(np.pad(nsa_overlap(ncp, ns), ((0, 0), (0, nsp - ns))))
    args = [qt]
    in_specs = [pl.BlockSpec((1, DEC_HEADS, 4 * HEAD_DIM), lambda bi, j, pt: (bi, 0, 0))]
    for pool in (pool_k, pool_v):
        for half in range(2):
            for i in range(n_blk):
                args.append(pool)
                in_specs.append(pl.BlockSpec((None, pool.shape[1], LANES),
                                             lambda bi, j, pt, i=i, half=half: (pt[bi, j * n_blk + i], 0, half)))
    for a in tuple(cw_k) + tuple(cw_v) + (ov,):
        args.append(a)
        in_specs.append(pl.BlockSpec(a.shape, lambda bi, j, pt, nd=a.ndim: (0,) * nd))
    return pl.pallas_call(
        functools.partial(_nsa_cmp_decode_kernel, n_blk=n_blk, ncp=ncp, ns=ns, past=past),
        out_shape=[jax.ShapeDtypeStruct((b, DEC_HEADS, 4 * HEAD_DIM), F32),
                   jax.ShapeDtypeStruct((b, 8, nsp), F32)],
        grid_spec=pltpu.PrefetchScalarGridSpec(
            num_scalar_prefetch=1, grid=(b, n_pages // n_blk), in_specs=in_specs,
            out_specs=[pl.BlockSpec((1, DEC_HEADS, 4 * HEAD_DIM), lambda bi, j, pt: (bi, 0, 0)),
                       pl.BlockSpec((1, 8, nsp), lambda bi, j, pt: (bi, 0, 0))],
            scratch_shapes=[pltpu.VMEM((NSA_CMP_STRIDE, ncp, 4 * HEAD_DIM), F32)] * 2),
        compiler_params=_ARB(2), name="nsa_cmp_decode",
    )(page_table, *args)


def gather_pages(pool, page_table):
    rows = pool[page_table]
    return rows.reshape((page_table.shape[0], page_table.shape[1] * pool.shape[1]) + pool.shape[2:])


def _pad_cols(w, n):
    return jnp.pad(w, ((0, 0), (0, n - w.shape[1])))


def _pad_vec(v, n):
    return jnp.pad(v, (0, n - v.shape[0]))


def fox_mixer(hp, hs, g, cache_k, cache_v, cache_logf, page_table, w_in, b_f, w_o):
    bp, sp, d = hp.shape
    bs, ds, _ = hs.shape
    past = page_table.shape[1] * cache_k.shape[1]
    hd = FOX_HEADS * HEAD_DIM
    wb = w_in.astype(BF16)
    ws = [wb[:, :hd], wb[:, hd:2 * hd], wb[:, 2 * hd:3 * hd], _pad_cols(wb[:, 3 * hd:], LANES)]
    biases = [None, None, None, _pad_vec(b_f, LANES)]
    wo = w_o.astype(BF16)
    hp2 = hp.reshape(bp * sp, d)
    hs2 = hs.reshape(bs * ds, d)
    cfgs = [ProjCfg(f32=False, bf16=True, scale=SCALE), ProjCfg(bf16=True), ProjCfg(bf16=True),
            ProjCfg(act="logsig")]
    q16, k, k16, v, v16, lf = norm_proj(hp2, g, ws, cfgs, biases)
    logf = lf[:, :FOX_HEADS].reshape(bp, sp, FOX_HEADS)
    c = jnp.cumsum(logf, axis=1)
    o_p = fox_prompt_attention(q16.reshape(bp, sp, hd), k16.reshape(bp, sp, hd), v16.reshape(bp, sp, hd), c)
    hp_new = out_proj([o_p.reshape(bp * sp, hd)], wo, hp2).reshape(bp, sp, d)
    sh = (bp, sp, FOX_HEADS, HEAD_DIM)
    k, v = k.reshape(sh), v.reshape(sh)
    cfgs = [ProjCfg(), ProjCfg(), ProjCfg(), ProjCfg(act="logsig")]
    qs, ks, vs, lfs = norm_proj(hs2, g, ws, cfgs, biases)
    logfs = lfs[:, :FOX_HEADS].reshape(bs, ds, FOX_HEADS)
    c_all = jnp.cumsum(jnp.concatenate([gather_pages(cache_logf, page_table).astype(F32), logfs], axis=1), axis=1)
    bias = (c_all[:, past:] - c_all[:, :past]).transpose(0, 2, 1)
    hsh = (bs, FOX_HEADS, HEAD_DIM)
    o_s = decode_attention_keylane((qs * SCALE).reshape(bs, FOX_HEADS, 1, HEAD_DIM), _keys_on_lanes(cache_k),
                                   _keys_on_lanes(cache_v), bias, ks.reshape(hsh), vs.reshape(hsh), page_table)
    hs_new = out_proj([o_s.reshape(bs * ds, D_MODEL)], wo, hs2).reshape(bs, ds, d)
    shs = (bs, ds, FOX_HEADS, HEAD_DIM)
    ks, vs = ks.reshape(shs), vs.reshape(shs)
    return hp_new, hs_new, (k, v, logf, ks, vs, logfs)


def mlp_both(hp, hs, g, wu, wd):
    wu = wu.astype(BF16)
    wd = wd.astype(BF16)
    yp = mlp(hp.reshape(-1, D_MODEL), g, wu, wd).reshape(hp.shape)
    ys = mlp(hs.reshape(-1, D_MODEL), g, wu, wd).reshape(hs.shape)
    return yp, ys


def nsa_mixer(hp, hs, gn, cache_cmp_k, cache_cmp_v, cache_sel_k, cache_sel_v, state_win_k, state_win_v,
              page_table, w_in, b_gate, pe_k, w1_k, w2_k, pe_v, w1_v, w2_v, w_o):
    bp, sp, d = hp.shape
    bs, ds, _ = hs.shape
    past = page_table.shape[1] * cache_cmp_k.shape[1]
    wb_ = state_win_k.shape[1]
    hq = NSA_HEADS * HEAD_DIM
    kvd = NSA_GROUPS * HEAD_DIM
    wbf = w_in.astype(BF16)
    ws = [wbf[:, :hq]] + [wbf[:, hq + i * kvd:hq + (i + 1) * kvd] for i in range(6)] \
        + [_pad_cols(wbf[:, hq + 6 * kvd:], LANES)]
    biases = [None] * 7 + [_pad_vec(b_gate, LANES)]
    wo = w_o.astype(BF16)
    hp2 = hp.reshape(bp * sp, d)
    hs2 = hs.reshape(bs * ds, d)
    cfgs = [ProjCfg(rope=True, f32=False, bf16=True, scale=SCALE),
            ProjCfg(rope=True), ProjCfg(),
            ProjCfg(rope=True, bf16=True), ProjCfg(bf16=True),
            ProjCfg(rope=True, bf16=True), ProjCfg(bf16=True),
            ProjCfg(act="sigmoid")]
    tabs = rope_tables(jnp.arange(sp, dtype=jnp.int32))
    (q16, kc, vc, ksl, ksl16, vsl, vsl16, kw, kw16, vw, vw16, gt) = norm_proj(hp2, gn, ws, cfgs, biases, tabs)
    r3 = lambda a: a.reshape(bp, sp, a.shape[-1])
    q16 = r3(q16)
    gates = gt[:, :3 * NSA_HEADS].reshape(bp, sp, 3, NSA_HEADS)
    g_cmp = gates[:, :, 0].reshape(bp, sp, NSA_GROUPS, NSA_REP).transpose(0, 2, 1, 3)
    g_sel = gates[:, :, 1].reshape(bp, sp, 2, 8).transpose(0, 2, 3, 1)
    g_win = gates[:, :, 2].reshape(bp, sp, 2, 8).transpose(0, 2, 3, 1)
    kcc = nsa_compress_prompt(r3(kc), nsa_compress_weights(pe_k, w1_k, w2_k))
    vcc = nsa_compress_prompt(r3(vc), nsa_compress_weights(pe_v, w1_v, w2_v))
    o_cmp, sel = nsa_cmp_prompt(q16, kcc, vcc, g_cmp)
    o_sel = nsa_branch_prompt(q16, r3(ksl16), r3(vsl16), g_sel, sel)
    o_win = nsa_branch_prompt(q16, r3(kw16), r3(vw16), g_win)
    hp_new = out_proj([o_cmp.reshape(bp * sp, d), o_sel.reshape(bp * sp, d), o_win.reshape(bp * sp, d)],
                      wo, hp2).reshape(bp, sp, d)
    r4 = lambda a: a.reshape(bp, sp, NSA_GROUPS, HEAD_DIM)
    kc, vc, ksl, vsl, kw, vw = r4(kc), r4(vc), r4(ksl), r4(vsl), r4(kw), r4(vw)
    cfgs = [ProjCfg(rope=True), ProjCfg(rope=True), ProjCfg(), ProjCfg(rope=True), ProjCfg(),
            ProjCfg(rope=True), ProjCfg(), ProjCfg(act="sigmoid")]
    tabs_s = rope_tables(jnp.full((bs * ds,), past, jnp.int32))
    qs, kc_s, vc_s, ksl_s, vsl_s, kw_s, vw_s, gt_s = norm_proj(hs2, gn, ws, cfgs, biases, tabs_s)
    r4s = lambda a: a.reshape(bs, ds, NSA_GROUPS, HEAD_DIM)
    kc_s, vc_s, ksl_s, vsl_s, kw_s, vw_s = r4s(kc_s), r4s(vc_s), r4s(ksl_s), r4s(vsl_s), r4s(kw_s), r4s(vw_s)
    gates_s = gt_s[:, :3 * NSA_HEADS].reshape(bs, ds, 3, NSA_HEADS)
    kvd4 = NSA_GROUPS * HEAD_DIM
    pool = lambda a: a.reshape(a.shape[0], a.shape[1], kvd4)
    row = lambda a: a.reshape(bs, 1, kvd4)
    qt = _spread_heads(qs.reshape(bs, NSA_HEADS, HEAD_DIM), kvd4)
    o_cmp_full, selm = nsa_cmp_decode(qt, pool(cache_cmp_k), pool(cache_cmp_v), page_table,
                                      nsa_compress_weights(pe_k, w1_k, w2_k), nsa_compress_weights(pe_v, w1_v, w2_v))
    n_past_blk = past // NSA_SEL_LEN
    sel_keys = jnp.repeat(selm[:, :NSA_GROUPS, :n_past_blk], NSA_SEL_LEN, axis=2)
    bias_sel = jnp.where(sel_keys > 0.5, 0.0, NEG)
    gsh = (bs, NSA_GROUPS, HEAD_DIM)
    o_sel = decode_attention_keylane((qs * SCALE).reshape(bs, NSA_GROUPS, NSA_REP, HEAD_DIM), _keys_on_lanes(cache_sel_k),
                                     _keys_on_lanes(cache_sel_v), bias_sel, ksl_s.reshape(gsh), vsl_s.reshape(gsh),
                                     page_table).reshape(bs, NSA_HEADS, HEAD_DIM)
    in_win = jnp.arange(wb_) > wb_ - NSA_WINDOW
    bias_win = jnp.broadcast_to(jnp.where(in_win, 0.0, NEG)[None, None, :], (bs, NSA_HEADS, wb_)).astype(F32)
    o_win_full = decode_attention(qt, state_win_k.reshape(bs, wb_, kvd4), state_win_v.reshape(bs, wb_, kvd4),
                                  bias_win, row(kw_s), row(vw_s))
    gs = gates_s.reshape(bs, 3, NSA_HEADS, 1)
    o_s = (gs[:, 0] * _own_lanes(o_cmp_full, HEAD_DIM) + gs[:, 1] * o_sel
           + gs[:, 2] * _own_lanes(o_win_full, HEAD_DIM))
    hs_new = out_proj([o_s.reshape(bs * ds, d)], wo, hs2).reshape(bs, ds, d)
    kw_all = jnp.concatenate([state_win_k, kw_s], axis=1)
    vw_all = jnp.concatenate([state_win_v, vw_s], axis=1)
    keep = min(NSA_WINDOW, sp)
    return hp_new, hs_new, (kc, vc, ksl, vsl, kw[:, sp - keep:], vw[:, sp - keep:],
                            kc_s, vc_s, ksl_s, vsl_s, kw_all[:, ds:], vw_all[:, ds:])


def diff_mixer(hp, hs, gn, cache_k, cache_v, page_table, w_in, lq1, lk1, lq2, lk2, subln, w_o, layer_idx):
    lam_init = 0.8 - 0.6 * math.exp(-0.3 * layer_idx)
    lam = (jnp.exp(jnp.sum(lq1.astype(F32) * lk1.astype(F32)))
           - jnp.exp(jnp.sum(lq2.astype(F32) * lk2.astype(F32))) + lam_init)
    bp, sp, d = hp.shape
    bs, ds, _ = hs.shape
    past = page_table.shape[1] * cache_k.shape[1]
    wbf = w_in.astype(BF16)
    ws = [wbf[:, :d], wbf[:, d:2 * d], wbf[:, 2 * d:]]
    wo = w_o.astype(BF16)
    hp2 = hp.reshape(bp * sp, d)
    hs2 = hs.reshape(bs * ds, d)
    cfgs = [ProjCfg(rope=True, f32=False, bf16=True, scale=SCALE), ProjCfg(rope=True, bf16=True),
            ProjCfg(bf16=True)]
    tabs = rope_tables(jnp.arange(sp, dtype=jnp.int32))
    q16, k, k16, v, v16 = norm_proj(hp2, gn, ws, cfgs, None, tabs)
    r3 = lambda a: a.reshape(bp, sp, d)
    o_p = diff_prompt_attention(r3(q16), r3(k16), r3(v16), lam, lam_init, subln)
    hp_new = out_proj([o_p.reshape(bp * sp, d)], wo, hp2).reshape(bp, sp, d)
    k = k.reshape(bp, sp, 2 * DIFF_HEADS, HEAD_DIM)
    v = v.reshape(bp, sp, DIFF_HEADS, 2 * HEAD_DIM)
    cfgs = [ProjCfg(rope=True), ProjCfg(rope=True), ProjCfg()]
    tabs_s = rope_tables(jnp.full((bs * ds,), past, jnp.int32))
    qs, ks, vs = norm_proj(hs2, gn, ws, cfgs, None, tabs_s)
    pool = lambda a: a.reshape(a.shape[0], a.shape[1], d)
    o_full = decode_attention(_spread_heads(qs.reshape(bs, 2 * DIFF_HEADS, HEAD_DIM), d), pool(cache_k), pool(cache_v),
                              None, ks.reshape(bs, 1, d), vs.reshape(bs, 1, d), page_table)
    o_sub = _own_lanes(o_full, 2 * HEAD_DIM)
    o_s = diff_finish(o_sub[:, 0::2].reshape(bs * DIFF_HEADS, 2 * HEAD_DIM),
                      o_sub[:, 1::2].reshape(bs * DIFF_HEADS, 2 * HEAD_DIM), lam, lam_init, subln)
    hs_new = out_proj([o_s.reshape(bs * ds, D_MODEL)], wo, hs2).reshape(bs, ds, d)
    ks = ks.reshape(bs, ds, 2 * DIFF_HEADS, HEAD_DIM)
    vs = vs.reshape(bs, ds, DIFF_HEADS, 2 * HEAD_DIM)
    return hp_new, hs_new, (k, v, ks, vs)


def kernel(x_prompt, x_sample, cache_l0_k, cache_l0_v, cache_l0_logf, cache_l1_cmp_k, cache_l1_cmp_v, cache_l1_sel_k, cache_l1_sel_v, state_l1_win_k, state_l1_win_v, cache_l2_k, cache_l2_v, cache_l3_k, cache_l3_v, cache_l3_logf, page_table, l0_norm_mix, l0_fox_w_in, l0_fox_b_f, l0_fox_w_o, l0_norm_mlp, l0_mlp_up, l0_mlp_down, l1_norm_mix, l1_nsa_w_in, l1_nsa_b_gate, l1_nsa_pe_k, l1_nsa_w1_k, l1_nsa_w2_k, l1_nsa_pe_v, l1_nsa_w1_v, l1_nsa_w2_v, l1_nsa_w_o, l1_norm_mlp, l1_mlp_up, l1_mlp_down, l2_norm_mix, l2_diff_w_in, l2_diff_lq1, l2_diff_lk1, l2_diff_lq2, l2_diff_lk2, l2_diff_subln, l2_diff_w_o, l2_norm_mlp, l2_mlp_up, l2_mlp_down, l3_norm_mix, l3_fox_w_in, l3_fox_b_f, l3_fox_w_o, l3_norm_mlp, l3_mlp_up, l3_mlp_down, norm_final):
    hp, hs = x_prompt, x_sample
    state = []
    hp, hs, st = fox_mixer(hp, hs, l0_norm_mix, cache_l0_k, cache_l0_v, cache_l0_logf, page_table,
                           l0_fox_w_in, l0_fox_b_f, l0_fox_w_o)
    state += st
    hp, hs = mlp_both(hp, hs, l0_norm_mlp, l0_mlp_up, l0_mlp_down)
    hp, hs, st = nsa_mixer(hp, hs, l1_norm_mix, cache_l1_cmp_k, cache_l1_cmp_v, cache_l1_sel_k, cache_l1_sel_v,
                           state_l1_win_k, state_l1_win_v, page_table,
                           l1_nsa_w_in, l1_nsa_b_gate, l1_nsa_pe_k, l1_nsa_w1_k, l1_nsa_w2_k,
                           l1_nsa_pe_v, l1_nsa_w1_v, l1_nsa_w2_v, l1_nsa_w_o)
    state += st
    hp, hs = mlp_both(hp, hs, l1_norm_mlp, l1_mlp_up, l1_mlp_down)
    hp, hs, st = diff_mixer(hp, hs, l2_norm_mix, cache_l2_k, cache_l2_v, page_table, l2_diff_w_in,
                            l2_diff_lq1, l2_diff_lk1, l2_diff_lq2, l2_diff_lk2,
                            l2_diff_subln, l2_diff_w_o, DIFF_LAYER)
    state += st
    hp, hs = mlp_both(hp, hs, l2_norm_mlp, l2_mlp_up, l2_mlp_down)
    hp, hs, st = fox_mixer(hp, hs, l3_norm_mix, cache_l3_k, cache_l3_v, cache_l3_logf, page_table,
                           l3_fox_w_in, l3_fox_b_f, l3_fox_w_o)
    state += st
    hp, hs = mlp_both(hp, hs, l3_norm_mlp, l3_mlp_up, l3_mlp_down)
    y_p = final_norm(hp.reshape(-1, D_MODEL), norm_final).reshape(hp.shape)
    y_s = final_norm(hs.reshape(-1, D_MODEL), norm_final).reshape(hs.shape)
    return (y_p, y_s) + tuple(state)
```

```python
import functools
import math
from typing import NamedTuple

import numpy as np
import jax
import jax.numpy as jnp
from jax import lax
from jax.experimental import pallas as pl
from jax.experimental.pallas import tpu as pltpu

F32 = jnp.float32
BF16 = jnp.bfloat16

D_MODEL = 1024
HEAD_DIM = 64
FOX_HEADS = 16
NSA_HEADS = 16
NSA_GROUPS = 4
NSA_REP = 4
NSA_CMP_LEN = 32
NSA_CMP_STRIDE = 16
NSA_SEL_LEN = 64
NSA_TOP_N = 16
NSA_N_LOCAL = 2
NSA_WINDOW = 512
DIFF_HEADS = 8
D_FF = 4096
ROPE_THETA = 10000.0
NORM_EPS = 1e-6
DIFF_LAYER = 2
SCALE = HEAD_DIM ** -0.5

LANES = 128
VMEM_LIMIT = 56 * 1024 * 1024
ROW_TILE = 512
NEG = -1e30

_ARB = lambda n: pltpu.CompilerParams(dimension_semantics=("arbitrary",) * n,
                                      vmem_limit_bytes=VMEM_LIMIT)


def _row_tile(m):
    return ROW_TILE if m % ROW_TILE == 0 else m


def _rms(x, g):
    return x * lax.rsqrt(jnp.mean(x * x, axis=-1, keepdims=True) + NORM_EPS) * g


def _swap_halves(x):
    parts = []
    for c in range(x.shape[-1] // LANES):
        parts += [x[:, c * LANES + 64:(c + 1) * LANES], x[:, c * LANES:c * LANES + 64]]
    return jnp.concatenate(parts, axis=-1)


class ProjCfg(NamedTuple):
    rope: bool = False
    f32: bool = True
    bf16: bool = False
    scale: float = 1.0
    act: str = ""


def _rope(y, cos, sin):
    n = y.shape[-1]
    reps = n // LANES
    if reps > 1:
        cos = jnp.concatenate([cos] * reps, axis=-1)
        sin = jnp.concatenate([sin] * reps, axis=-1)
    lane = lax.broadcasted_iota(jnp.int32, y.shape, 1)
    partner = jnp.where((lane % HEAD_DIM) < HEAD_DIM // 2,
                        pltpu.roll(y, n - HEAD_DIM // 2, 1), pltpu.roll(y, HEAD_DIM // 2, 1))
    return y * cos + partner * sin


def _norm_proj_kernel(*refs, cfgs, use_rope):
    it = iter(refs)
    x_ref, g_ref = next(it), next(it)
    if use_rope:
        cos_ref, sin_ref = next(it), next(it)
    ins = []
    for c in cfgs:
        w_ref = next(it)
        ins.append((w_ref, next(it) if c.act else None))
    xn = _rms(x_ref[...], g_ref[...]).astype(BF16)
    for c, (w_ref, b_ref) in zip(cfgs, ins):
        y = jnp.dot(xn, w_ref[...], preferred_element_type=F32)
        if c.rope:
            y = _rope(y, cos_ref[...], sin_ref[...])
        if c.act:
            z = y + b_ref[...]
            if c.act == "logsig":
                y = jnp.minimum(z, 0.0) - jnp.log1p(jnp.exp(-jnp.abs(z)))
            else:
                y = 1.0 / (1.0 + jnp.exp(-z))
        if c.f32:
            next(it)[...] = y
        if c.bf16:
            next(it)[...] = (y * c.scale).astype(BF16)


def norm_proj(x, g, ws, cfgs, biases=None, rope_tables=None):
    m, d = x.shape
    tm = _row_tile(m)
    use_rope = rope_tables is not None
    args = [x, g.reshape(1, d)]
    in_specs = [pl.BlockSpec((tm, d), lambda i: (i, 0)), pl.BlockSpec((1, d), lambda i: (0, 0))]
    if use_rope:
        nt = rope_tables[0].shape[0] // tm
        for t in rope_tables:
            args.append(t)
            in_specs.append(pl.BlockSpec((tm, LANES), lambda i: (i % nt, 0)))
    out_specs, out_shape = [], []
    for ci, (w, c) in enumerate(zip(ws, cfgs)):
        n = w.shape[1]
        args.append(w)
        in_specs.append(pl.BlockSpec(w.shape, lambda i: (0, 0)))
        if c.act:
            args.append(biases[ci].reshape(1, n))
            in_specs.append(pl.BlockSpec((1, n), lambda i: (0, 0)))
        for flag, dt in ((c.f32, F32), (c.bf16, BF16)):
            if flag:
                out_specs.append(pl.BlockSpec((tm, n), lambda i: (i, 0)))
                out_shape.append(jax.ShapeDtypeStruct((m, n), dt))
    return pl.pallas_call(
        functools.partial(_norm_proj_kernel, cfgs=tuple(cfgs), use_rope=use_rope),
        grid=(m // tm,), in_specs=in_specs, out_specs=out_specs, out_shape=out_shape,
        compiler_params=_ARB(1), name="norm_proj",
    )(*args)


def _out_proj_kernel(*refs, n_in):
    a_refs = refs[:n_in]
    w_ref, res_ref, o_ref = refs[n_in:]
    a = a_refs[0][...]
    for r in a_refs[1:]:
        a = a + r[...]
    o_ref[...] = res_ref[...] + jnp.dot(a.astype(BF16), w_ref[...], preferred_element_type=F32)


def out_proj(a_list, w, res):
    m, d = res.shape
    tm = _row_tile(m)
    k = w.shape[0]
    return pl.pallas_call(
        functools.partial(_out_proj_kernel, n_in=len(a_list)),
        grid=(m // tm,),
        in_specs=[pl.BlockSpec((tm, k), lambda i: (i, 0)) for _ in a_list]
                 + [pl.BlockSpec(w.shape, lambda i: (0, 0)),
                    pl.BlockSpec((tm, d), lambda i: (i, 0))],
        out_specs=pl.BlockSpec((tm, d), lambda i: (i, 0)),
        out_shape=jax.ShapeDtypeStruct((m, d), F32),
        compiler_params=_ARB(1), name="out_proj",
    )(*a_list, w, res)


FF_CHUNK = 1024


def _mlp_kernel(x_ref, g_ref, wu_ref, wd_ref, o_ref):
    x = x_ref[...]
    xn = _rms(x, g_ref[...]).astype(BF16)
    acc = x
    for c in range(D_FF // FF_CHUNK):
        h = jnp.dot(xn, wu_ref[:, c * FF_CHUNK:(c + 1) * FF_CHUNK], preferred_element_type=F32)
        h = jnp.maximum(h, 0.0)
        acc = acc + jnp.dot((h * h).astype(BF16), wd_ref[c * FF_CHUNK:(c + 1) * FF_CHUNK, :],
                            preferred_element_type=F32)
    o_ref[...] = acc


def mlp(x, g, wu, wd):
    m, d = x.shape
    tm = _row_tile(m)
    return pl.pallas_call(
        _mlp_kernel,
        grid=(m // tm,),
        in_specs=[pl.BlockSpec((tm, d), lambda i: (i, 0)),
                  pl.BlockSpec((1, d), lambda i: (0, 0)),
                  pl.BlockSpec(wu.shape, lambda i: (0, 0)),
                  pl.BlockSpec(wd.shape, lambda i: (0, 0))],
        out_specs=pl.BlockSpec((tm, d), lambda i: (i, 0)),
        out_shape=jax.ShapeDtypeStruct((m, d), F32),
        compiler_params=_ARB(1), name="mlp",
    )(x, g.reshape(1, d), wu, wd)


def _final_norm_kernel(x_ref, g_ref, o_ref):
    o_ref[...] = _rms(x_ref[...], g_ref[...])


def final_norm(x, g):
    m, d = x.shape
    tm = _row_tile(m)
    return pl.pallas_call(
        _final_norm_kernel,
        grid=(m // tm,),
        in_specs=[pl.BlockSpec((tm, d), lambda i: (i, 0)),
                  pl.BlockSpec((1, d), lambda i: (0, 0))],
        out_specs=pl.BlockSpec((tm, d), lambda i: (i, 0)),
        out_shape=jax.ShapeDtypeStruct((m, d), F32),
        compiler_params=_ARB(1), name="final_norm",
    )(x, g.reshape(1, d))


def rope_tables(pos):
    half = HEAD_DIM // 2
    inv = ROPE_THETA ** (-jnp.arange(half, dtype=F32) / half)
    ang = pos.astype(F32)[:, None] * inv[None, :]
    cos, sin = jnp.cos(ang), jnp.sin(ang)
    cos = jnp.concatenate([cos, cos, cos, cos], axis=-1)
    sin = jnp.concatenate([-sin, sin, -sin, sin], axis=-1)
    return cos, sin


def _flash_streams(qs, load_kv, j_lo, j_diag, mask_fn, robust, row_bias=None, col_bias_fn=None):
    n = len(qs)
    cols = qs[0].shape[0]

    def step(j, carry, diag):
        k, vt = load_kv(j)
        out = []
        for i in range(n):
            m, l, acc = carry[i]
            s = lax.dot_general(k, qs[i], (((1,), (1,)), ((), ())), preferred_element_type=F32)
            if col_bias_fn is not None:
                s = s + col_bias_fn(i, j)
            msk = mask_fn(i, j, diag) if mask_fn is not None else None
            if msk is not None:
                s = jnp.where(msk, s, NEG)
            rmax = jnp.max(s, axis=0, keepdims=True)
            if row_bias is not None:
                rmax = rmax + row_bias[i]
            m_new = jnp.maximum(m, rmax)
            shift = m_new - row_bias[i] if row_bias is not None else m_new
            alpha = jnp.exp(m - m_new)
            p = jnp.exp(s - shift)
            if msk is not None and robust:
                p = jnp.where(msk, p, 0.0)
            l = alpha * l + jnp.sum(p, axis=0, keepdims=True)
            acc = alpha * acc + jnp.dot(vt, p.astype(BF16), preferred_element_type=F32)
            out.append((m_new, l, acc))
        return tuple(out)

    dv = load_kv(0)[1].shape[0]
    one = (jnp.full((1, cols), NEG, F32), jnp.zeros((1, cols), F32), jnp.zeros((dv, cols), F32))
    carry = lax.fori_loop(j_lo, j_diag, lambda j, c: step(j, c, False), (one,) * n)
    return [acc / jnp.maximum(l, 1e-30) for _, l, acc in step(j_diag, carry, True)]


def _causal_mask(q0, tq, cols, j, tk):
    kpos = j * tk + lax.broadcasted_iota(jnp.int32, (tk, cols), 0)
    qpos = q0 + lax.broadcasted_iota(jnp.int32, (tk, cols), 1) % tq
    return qpos, kpos


def _chunked_t(x, tk):
    b, s, w = x.shape
    return x.reshape(b, s // tk, tk, w // LANES, LANES).transpose(0, 3, 1, 4, 2)


def _head_pair_kernel(*refs, tq, tk, mode, lam_init):
    if mode == "fox":
        q_ref, k_ref, vt_ref, cq_ref, ck_ref, o_ref = refs
    else:
        q_ref, k_ref, vt_ref, lam_ref, subln_ref, o_ref = refs
    q0 = pl.program_id(2) * tq
    q = q_ref[0]
    lane = lax.broadcasted_iota(jnp.int32, (tq, LANES), 1)
    zero = jnp.zeros_like(q)

    def load_kv(j):
        st = pl.multiple_of(j * tk, tk)
        return k_ref[0, pl.ds(st, tk), :], vt_ref[0, 0, j]

    def mask_fn(i, j, diag):
        if not diag:
            return None
        qpos, kpos = _causal_mask(q0, tq, tq, j, tk)
        return kpos <= qpos

    qs = [jnp.where((lane >= 64 * i) & (lane < 64 * (i + 1)), q, zero) for i in range(2)]
    if mode == "fox":
        outs = _flash_streams(qs, load_kv, 0, q0 // tk, mask_fn, False,
                              row_bias=[cq_ref[0, 0, i:i + 1, :] for i in range(2)],
                              col_bias_fn=lambda i, j: -ck_ref[0, 0, j][:, i:i + 1])
        sub = lax.broadcasted_iota(jnp.int32, outs[0].shape, 0)
        ot = jnp.where(sub < 64, outs[0], outs[1])
    else:
        outs = _flash_streams(qs, load_kv, 0, q0 // tk, mask_fn, False)
        ot = outs[0] - lam_ref[...] * outs[1]
        ms = jnp.mean(ot * ot, axis=0, keepdims=True)
        ot = ot * lax.rsqrt(ms + NORM_EPS) * subln_ref[...] * (1.0 - lam_init)
    o_ref[0] = ot.T.astype(o_ref.dtype)


ATT_TQ = 512
ATT_TK = 1024
ATT_TQ_NSA = 256


def fox_prompt_attention(q, k, v, c):
    b, s, d = q.shape
    tq, tk = min(ATT_TQ, s), min(ATT_TK, s)
    hp = d // LANES
    cq = c.reshape(b, s, hp, 2).transpose(0, 2, 3, 1)
    ck = c.reshape(b, s // tk, tk, hp, 2).transpose(0, 3, 1, 2, 4)
    return pl.pallas_call(
        functools.partial(_head_pair_kernel, tq=tq, tk=tk, mode="fox", lam_init=0.0),
        grid=(b, hp, s // tq),
        in_specs=[pl.BlockSpec((1, tq, LANES), lambda bi, h, qi: (bi, qi, h)),
                  pl.BlockSpec((1, s, LANES), lambda bi, h, qi: (bi, 0, h)),
                  pl.BlockSpec((1, 1, s // tk, LANES, tk), lambda bi, h, qi: (bi, h, 0, 0, 0)),
                  pl.BlockSpec((1, 1, 2, tq), lambda bi, h, qi: (bi, h, 0, qi)),
                  pl.BlockSpec((1, 1, s // tk, tk, 2), lambda bi, h, qi: (bi, h, 0, 0, 0))],
        out_specs=pl.BlockSpec((1, tq, LANES), lambda bi, h, qi: (bi, qi, h)),
        out_shape=jax.ShapeDtypeStruct((b, s, d), BF16),
        compiler_params=_ARB(3), name="fox_prompt_attention",
    )(q, k, _chunked_t(v, tk), cq, ck)


def diff_prompt_attention(q, k, v, lam, lam_init, subln):
    b, s, d = q.shape
    tq, tk = min(ATT_TQ, s), min(ATT_TK, s)
    hp = d // LANES
    lam_vec = jnp.broadcast_to(lam.astype(F32).reshape(1, 1), (1, tq))
    return pl.pallas_call(
        functools.partial(_head_pair_kernel, tq=tq, tk=tk, mode="diff", lam_init=lam_init),
        grid=(b, hp, s // tq),
        in_specs=[pl.BlockSpec((1, tq, LANES), lambda bi, h, qi: (bi, qi, h)),
                  pl.BlockSpec((1, s, LANES), lambda bi, h, qi: (bi, 0, h)),
                  pl.BlockSpec((1, 1, s // tk, LANES, tk), lambda bi, h, qi: (bi, h, 0, 0, 0)),
                  pl.BlockSpec((1, tq), lambda bi, h, qi: (0, 0)),
                  pl.BlockSpec((LANES, 1), lambda bi, h, qi: (0, 0))],
        out_specs=pl.BlockSpec((1, tq, LANES), lambda bi, h, qi: (bi, qi, h)),
        out_shape=jax.ShapeDtypeStruct((b, s, d), BF16),
        compiler_params=_ARB(3), name="diff_prompt_attention",
    )(q, k, _chunked_t(v, tk), lam_vec, subln.reshape(LANES, 1))


def _compress_core(get_phase, pe_ref, w1_ref, w2_ref, nc):
    lo = jnp.zeros((nc, 4 * HEAD_DIM), F32)
    hi = jnp.zeros((nc, 4 * HEAD_DIM), F32)
    for l in range(NSA_CMP_STRIDE):
        ph = get_phase(l)
        lo = lo + jnp.dot((ph + pe_ref[l:l + 1, :]).astype(BF16), w1_ref[l],
                          preferred_element_type=F32)
        hi = hi + jnp.dot((ph + pe_ref[l + NSA_CMP_STRIDE:l + NSA_CMP_STRIDE + 1, :]).astype(BF16),
                          w1_ref[l + NSA_CMP_STRIDE], preferred_element_type=F32)
    pre = lo + pltpu.roll(hi, nc - 1, 0)
    act = jax.nn.gelu(pre)
    return jnp.dot(act.astype(BF16), w2_ref[...], preferred_element_type=F32)


def _nsa_compress_kernel(rows_ref, pe_ref, w1_ref, w2_ref, o_ref, *, nc):
    out = _compress_core(lambda l: rows_ref[0, l], pe_ref, w1_ref, w2_ref, nc)
    for g in range(NSA_GROUPS):
        o_ref[0, g] = out[:, g * HEAD_DIM:(g + 1) * HEAD_DIM].astype(o_ref.dtype)


def _block_diag4(w):
    eye = jnp.eye(NSA_GROUPS, dtype=w.dtype)
    out = jnp.einsum("gh,...ij->...gihj", eye, w)
    return out.reshape(w.shape[:-2] + (NSA_GROUPS * HEAD_DIM, NSA_GROUPS * HEAD_DIM))


def nsa_compress_weights(pe, w1, w2):
    pe4 = jnp.tile(pe, (1, NSA_GROUPS))
    w1bd = _block_diag4(w1.reshape(NSA_CMP_LEN, HEAD_DIM, HEAD_DIM)).astype(BF16)
    w2bd = _block_diag4(w2).astype(BF16)
    return pe4, w1bd, w2bd


def nsa_compress_prompt(rows, cw):
    b, s, d = rows.shape
    nc = s // NSA_CMP_STRIDE
    pe4, w1bd, w2bd = cw
    return pl.pallas_call(
        functools.partial(_nsa_compress_kernel, nc=nc),
        grid=(b,),
        in_specs=[pl.BlockSpec((1, NSA_CMP_STRIDE, nc, d), lambda bi: (bi, 0, 0, 0)),
                  pl.BlockSpec(pe4.shape, lambda bi: (0, 0)),
                  pl.BlockSpec(w1bd.shape, lambda bi: (0, 0, 0)),
                  pl.BlockSpec(w2bd.shape, lambda bi: (0, 0))],
        out_specs=pl.BlockSpec((1, NSA_GROUPS, nc, HEAD_DIM), lambda bi: (bi, 0, 0, 0)),
        out_shape=jax.ShapeDtypeStruct((b, NSA_GROUPS, nc, HEAD_DIM), BF16),
        compiler_params=_ARB(1), name="nsa_compress",
    )(rows.reshape(b, nc, NSA_CMP_STRIDE, d).transpose(0, 2, 1, 3), pe4, w1bd, w2bd)


def _select_blocks(imp, pos, ns):
    blk = lax.broadcasted_iota(jnp.int32, imp.shape, 1)
    cur = pos // NSA_SEL_LEN
    valid = blk * NSA_SEL_LEN <= pos
    forced = (blk == 0) | ((blk <= cur) & (blk > cur - NSA_N_LOCAL))
    score = jnp.where(forced, jnp.inf, jnp.where(valid, imp, -jnp.inf))
    rank = jnp.zeros(imp.shape, jnp.int32)
    for i in range(ns):
        ci = score[:, i:i + 1]
        ahead = (ci > score) | ((ci == score) & (blk > i))
        rank = rank + ahead.astype(jnp.int32)
    return rank < NSA_TOP_N


def _select_blocks_t(imp, pos, ns):
    blk = lax.broadcasted_iota(jnp.int32, imp.shape, 0)
    cur = pos // NSA_SEL_LEN
    valid = blk * NSA_SEL_LEN <= pos
    forced = (blk == 0) | ((blk <= cur) & (blk > cur - NSA_N_LOCAL))
    score = jnp.where(forced, jnp.inf, jnp.where(valid, imp, -jnp.inf))
    rank = jnp.zeros(imp.shape, jnp.int32)
    for i in range(ns):
        ci = score[i:i + 1, :]
        ahead = (ci > score) | ((ci == score) & (blk > i))
        rank = rank + ahead.astype(jnp.int32)
    return rank < NSA_TOP_N


def _nsa_cmp_kernel(q_ref, kc_ref, vc_ref, ov_ref, gate_ref, o_ref, sel_ref, *, tq, nc, ns):
    q0 = pl.program_id(2) * tq
    q = q_ref[0]
    kc, vc = kc_ref[0, 0], vc_ref[0, 0]
    pos = q0 + lax.broadcasted_iota(jnp.int32, (tq, 1), 0)
    c_last = lax.broadcasted_iota(jnp.int32, (1, nc), 1) * NSA_CMP_STRIDE + (NSA_CMP_LEN - 1)
    valid = c_last <= pos
    gate = gate_ref[0, 0]
    psum = jnp.zeros((tq, nc), F32)
    outs = []
    for r in range(NSA_REP):
        qr = q[:, r * HEAD_DIM:(r + 1) * HEAD_DIM]
        s = lax.dot_general(qr, kc, (((1,), (1,)), ((), ())), preferred_element_type=F32)
        s = jnp.where(valid, s, NEG)
        e = jnp.where(valid, jnp.exp(s - jnp.max(s, axis=-1, keepdims=True)), 0.0)
        p = e / jnp.maximum(jnp.sum(e, axis=-1, keepdims=True), 1e-30)
        outs.append(jnp.dot(p.astype(BF16), vc, preferred_element_type=F32) * gate[:, r:r + 1])
        psum = psum + p
    o_ref[0] = jnp.concatenate(outs, axis=-1)
    imp = lax.dot_general(ov_ref[...], psum, (((1,), (1,)), ((), ())), preferred_element_type=F32,
                          precision=lax.Precision.HIGHEST)
    pos_row = q0 + lax.broadcasted_iota(jnp.int32, (1, tq), 1)
    sel = _select_blocks_t(imp, pos_row, ns)
    sel_ref[0, 0] = jnp.where(sel, 1.0, 0.0).astype(sel_ref.dtype)


def nsa_overlap(nc, ns):
    c0 = np.arange(nc)[:, None] * NSA_CMP_STRIDE
    s0 = np.arange(ns)[None, :] * NSA_SEL_LEN
    ov = np.minimum(c0 + NSA_CMP_LEN, s0 + NSA_SEL_LEN) - np.maximum(c0, s0)
    return (np.clip(ov, 0, None) / NSA_CMP_STRIDE).astype(np.float32)


def nsa_cmp_prompt(q, kc, vc, gate):
    b, s, d = q.shape
    tq = min(ATT_TQ_NSA, s)
    nc, ns = s // NSA_CMP_STRIDE, s // NSA_SEL_LEN
    gw = NSA_REP * HEAD_DIM
    ov = jnp.asarray(nsa_overlap(nc, ns).T)
    return pl.pallas_call(
        functools.partial(_nsa_cmp_kernel, tq=tq, nc=nc, ns=ns),
        grid=(b, NSA_GROUPS, s // tq),
        in_specs=[pl.BlockSpec((1, tq, gw), lambda bi, g, qi: (bi, qi, g)),
                  pl.BlockSpec((1, 1, nc, HEAD_DIM), lambda bi, g, qi: (bi, g, 0, 0)),
                  pl.BlockSpec((1, 1, nc, HEAD_DIM), lambda bi, g, qi: (bi, g, 0, 0)),
                  pl.BlockSpec((ns, nc), lambda bi, g, qi: (0, 0)),
                  pl.BlockSpec((1, 1, tq, NSA_REP), lambda bi, g, qi: (bi, g, qi, 0))],
        out_specs=[pl.BlockSpec((1, tq, gw), lambda bi, g, qi: (bi, qi, g)),
                   pl.BlockSpec((1, 1, ns, tq), lambda bi, g, qi: (bi, g, 0, qi))],
        out_shape=[jax.ShapeDtypeStruct((b, s, d), F32),
                   jax.ShapeDtypeStruct((b, NSA_GROUPS, ns, s), BF16)],
        compiler_params=_ARB(3), name="nsa_cmp_attention",
    )(q, kc, vc, ov, gate)


def _nsa_branch_kernel(*refs, tq, tk, mode):
    if mode == "sel":
        q_ref, k_ref, vt_ref, gate_ref, sel_ref, e_ref, o_ref = refs
    else:
        q_ref, k_ref, vt_ref, gate_ref, o_ref = refs
    q0 = pl.program_id(2) * tq
    q = q_ref[0]
    gate = gate_ref[0, 0]
    lane = lax.broadcasted_iota(jnp.int32, (tq, LANES), 1)
    cols = NSA_REP * tq

    def load_kv(j):
        st = pl.multiple_of(j * tk, tk)
        return k_ref[0, pl.ds(st, tk), :], vt_ref[0, 0, j]

    qs, sms = [], []
    for i in range(2):
        keep = (lane >= 64 * i) & (lane < 64 * (i + 1))
        parts = []
        for r in range(NSA_REP):
            c = 2 * i + r // 2
            x = q[:, c * LANES:(c + 1) * LANES]
            if r % 2 != i:
                x = _swap_halves(x)
            parts.append(jnp.where(keep, x, jnp.zeros_like(x)))
        qs.append(jnp.concatenate(parts, axis=0))
        if mode == "sel":
            penalty = (sel_ref[0, i].astype(F32) - 1.0) * (-NEG)
            sms.append(jnp.concatenate([penalty.astype(BF16)] * NSA_REP, axis=1))
    if mode == "sel":
        def mask_fn(i, j, diag):
            if not diag:
                return None
            qpos, kpos = _causal_mask(q0, tq, cols, j, tk)
            return kpos <= qpos
        bias_fn = lambda i, j: jnp.dot(e_ref[j], sms[i], preferred_element_type=F32)
        j_lo = 0
    else:
        def mask_fn(i, j, diag):
            qpos, kpos = _causal_mask(q0, tq, cols, j, tk)
            return (kpos <= qpos) & (qpos - kpos < NSA_WINDOW)
        bias_fn = None
        j_lo = jnp.maximum(q0 - (NSA_WINDOW - 1), 0) // tk
    outs = _flash_streams(qs, load_kv, j_lo, q0 // tk, mask_fn, True, col_bias_fn=bias_fn)
    pieces = []
    for i in range(2):
        for r in range(NSA_REP):
            hh = NSA_REP * i + r
            pieces.append(outs[i][64 * i:64 * (i + 1), r * tq:(r + 1) * tq] * gate[hh:hh + 1, :])
    o_ref[0] = jnp.concatenate(pieces, axis=0).T


SEL_TK = 512
WIN_TK = 256


def nsa_branch_prompt(q, k, v, gate, sel=None):
    b, s, d = q.shape
    mode = "sel" if sel is not None else "win"
    tq = min(ATT_TQ_NSA, s)
    tk = min(SEL_TK if mode == "sel" else WIN_TK, s)
    ns = s // NSA_SEL_LEN
    qw = 2 * NSA_REP * HEAD_DIM
    args = [q, k, _chunked_t(v, tk), gate]
    in_specs = [pl.BlockSpec((1, tq, qw), lambda bi, gp, qi: (bi, qi, gp)),
                pl.BlockSpec((1, s, LANES), lambda bi, gp, qi: (bi, 0, gp)),
                pl.BlockSpec((1, 1, s // tk, LANES, tk), lambda bi, gp, qi: (bi, gp, 0, 0, 0)),
                pl.BlockSpec((1, 1, 8, tq), lambda bi, gp, qi: (bi, gp, 0, qi))]
    if mode == "sel":
        kb = (np.arange(s) // NSA_SEL_LEN).reshape(s // tk, tk, 1)
        expand = jnp.asarray(kb == np.arange(ns).reshape(1, 1, ns), dtype=BF16)
        args += [sel, expand]
        in_specs += [pl.BlockSpec((1, 2, ns, tq), lambda bi, gp, qi: (bi, gp, 0, qi)),
                     pl.BlockSpec(expand.shape, lambda bi, gp, qi: (0, 0, 0))]
    return pl.pallas_call(
        functools.partial(_nsa_branch_kernel, tq=tq, tk=tk, mode=mode),
        grid=(b, 2, s // tq), in_specs=in_specs,
        out_specs=pl.BlockSpec((1, tq, qw), lambda bi, gp, qi: (bi, qi, gp)),
        out_shape=jax.ShapeDtypeStruct((b, s, d), F32),
        compiler_params=_ARB(3), name="nsa_%s_attention" % mode,
    )(*args)


PAGES_PER_STEP = 8
DEC_HEADS = 16


def _decode_attn_kernel(*refs, n_blk, blk_rows, has_bias, has_self, paged):
    it = iter(refs)
    if paged:
        next(it)
    qt_ref = next(it)
    bias_ref = next(it) if has_bias else None
    if has_self:
        ks_ref, vs_ref = next(it), next(it)
    k_refs = [next(it) for _ in range(n_blk)]
    v_refs = [next(it) for _ in range(n_blk)]
    o_ref = next(it)
    m_sc, l_sc, acc_sc = next(it), next(it), next(it)
    j = pl.program_id(1)

    @pl.when(j == 0)
    def _():
        m_sc[...] = jnp.full(m_sc.shape, NEG, F32)
        l_sc[...] = jnp.zeros(l_sc.shape, F32)
        acc_sc[...] = jnp.zeros(acc_sc.shape, F32)

    qt = qt_ref[0]
    m, l, acc = m_sc[...], l_sc[...], acc_sc[...]
    for i in range(n_blk):
        k = k_refs[i][...].astype(BF16)
        v = v_refs[i][...].astype(BF16)
        s = lax.dot_general(qt, k, (((1,), (1,)), ((), ())), preferred_element_type=F32)
        if has_bias:
            s = s + bias_ref[0, :, i * blk_rows:(i + 1) * blk_rows]
        m_new = jnp.maximum(m, jnp.max(s, axis=-1, keepdims=True))
        alpha = jnp.exp(m - m_new)
        p = jnp.exp(s - m_new)
        l = alpha * l + jnp.sum(p, axis=-1, keepdims=True)
        acc = alpha * acc + jnp.dot(p.astype(BF16), v, preferred_element_type=F32)
        m = m_new
    m_sc[...], l_sc[...], acc_sc[...] = m, l, acc

    @pl.when(j == pl.num_programs(1) - 1)
    def _():
        m, l, acc = m_sc[...], l_sc[...], acc_sc[...]
        if has_self:
            s = jnp.sum(qt.astype(F32) * ks_ref[0], axis=-1, keepdims=True)
            m_new = jnp.maximum(m, s)
            alpha = jnp.exp(m - m_new)
            p = jnp.exp(s - m_new)
            l = alpha * l + p
            acc = alpha * acc + p * vs_ref[0]
        o_ref[0] = acc / jnp.maximum(l, 1e-30)


def decode_attention(qt, k, v, bias=None, k_self=None, v_self=None, page_table=None):
    b, nh, lk = qt.shape
    lv = v.shape[-1]
    paged = page_table is not None
    has_bias, has_self = bias is not None, k_self is not None
    if paged:
        blk_rows, n_blk = k.shape[1], PAGES_PER_STEP
        steps = page_table.shape[1] // n_blk
    else:
        blk_rows, n_blk = min(k.shape[1], 512), 1
        steps = k.shape[1] // blk_rows
    pre = (lambda f: (lambda bi, j, pt: f(bi, j, pt))) if paged else (lambda f: (lambda bi, j: f(bi, j, None)))
    args = [qt]
    in_specs = [pl.BlockSpec((1, nh, lk), pre(lambda bi, j, pt: (bi, 0, 0)))]
    if has_bias:
        args.append(bias)
        in_specs.append(pl.BlockSpec((1, nh, n_blk * blk_rows), pre(lambda bi, j, pt: (bi, 0, j))))
    if has_self:
        args += [k_self, v_self]
        in_specs += [pl.BlockSpec((1, 1, lk), pre(lambda bi, j, pt: (bi, 0, 0))),
                     pl.BlockSpec((1, 1, lv), pre(lambda bi, j, pt: (bi, 0, 0)))]
    for arr, width in ((k, lk), (v, lv)):
        for i in range(n_blk):
            args.append(arr)
            if paged:
                in_specs.append(pl.BlockSpec((None, blk_rows, width),
                                             lambda bi, j, pt, i=i: (pt[bi, j * n_blk + i], 0, 0)))
            else:
                in_specs.append(pl.BlockSpec((None, blk_rows, width), lambda bi, j: (bi, j, 0)))
    out_spec = pl.BlockSpec((1, nh, lv), pre(lambda bi, j, pt: (bi, 0, 0)))
    scratch = [pltpu.VMEM((nh, 1), F32), pltpu.VMEM((nh, 1), F32), pltpu.VMEM((nh, lv), F32)]
    kern = functools.partial(_decode_attn_kernel, n_blk=n_blk, blk_rows=blk_rows,
                             has_bias=has_bias, has_self=has_self, paged=paged)
    out_shape = jax.ShapeDtypeStruct((b, nh, lv), F32)
    if paged:
        return pl.pallas_call(
            kern, out_shape=out_shape,
            grid_spec=pltpu.PrefetchScalarGridSpec(
                num_scalar_prefetch=1, grid=(b, steps), in_specs=in_specs, out_specs=out_spec,
                scratch_shapes=scratch),
            compiler_params=_ARB(2), name="decode_attention_paged",
        )(page_table, *args)
    return pl.pallas_call(
        kern, out_shape=out_shape, grid=(b, steps), in_specs=in_specs, out_specs=out_spec,
        scratch_shapes=scratch, compiler_params=_ARB(2), name="decode_attention",
    )(*args)


def _decode_keylane_kernel(pt_ref, q_ref, bias_ref, ks_ref, vs_ref, *rest, n_blk):
    k_refs, v_refs = rest[:n_blk], rest[n_blk:2 * n_blk]
    o_ref, m_sc, l_sc, acc_sc = rest[2 * n_blk:]
    j = pl.program_id(1)

    @pl.when(j == 0)
    def _():
        m_sc[...] = jnp.full(m_sc.shape, NEG, F32)
        l_sc[...] = jnp.zeros(l_sc.shape, F32)
        acc_sc[...] = jnp.zeros(acc_sc.shape, F32)

    q = q_ref[0]
    qb = jnp.broadcast_to(q, acc_sc.shape)
    m, l, acc = m_sc[...], l_sc[...], acc_sc[...]
    for i in range(n_blk):
        kt = k_refs[i][...][:, None]
        vt = v_refs[i][...][:, None]
        s = jnp.sum(qb * kt, axis=2, keepdims=True) + bias_ref[0, i]
        m_new = jnp.maximum(m, jnp.max(s, axis=-1, keepdims=True))
        alpha = jnp.exp(m - m_new)
        p = jnp.exp(s - m_new)
        l = alpha * l + jnp.sum(p, axis=-1, keepdims=True)
        acc = alpha * acc + p * vt
        m = m_new
    m_sc[...], l_sc[...], acc_sc[...] = m, l, acc

    @pl.when(j == pl.num_programs(1) - 1)
    def _():
        m, l, acc = m_sc[...], l_sc[...], acc_sc[...]
        s = jnp.sum(q * ks_ref[0], axis=2, keepdims=True)
        m_new = jnp.maximum(m, s)
        alpha = jnp.exp(m - m_new)
        p = jnp.exp(s - m_new)
        l = alpha * l + p
        o = alpha * jnp.sum(acc, axis=-1, keepdims=True) + p * vs_ref[0]
        o_ref[0] = o / jnp.maximum(l, 1e-30)


def decode_attention_keylane(q, k_pool, v_pool, bias, k_self, v_self, page_table):
    b, n_kv, rep, dh = q.shape
    n_pages = page_table.shape[1]
    psz = k_pool.shape[-1]
    n_blk = PAGES_PER_STEP
    bias6 = bias.reshape(b, n_kv, n_pages, 1, 1, psz).transpose(0, 2, 1, 3, 4, 5)
    args = [q.reshape(b, n_kv, rep, dh, 1), bias6, k_self.reshape(b, n_kv, 1, dh, 1), v_self.reshape(b, n_kv, 1, dh, 1)]
    in_specs = [pl.BlockSpec((1, n_kv, rep, dh, 1), lambda bi, j, pt: (bi, 0, 0, 0, 0)),
                pl.BlockSpec((1, n_blk, n_kv, 1, 1, psz), lambda bi, j, pt: (bi, j, 0, 0, 0, 0)),
                pl.BlockSpec((1, n_kv, 1, dh, 1), lambda bi, j, pt: (bi, 0, 0, 0, 0)),
                pl.BlockSpec((1, n_kv, 1, dh, 1), lambda bi, j, pt: (bi, 0, 0, 0, 0))]
    for pool in (k_pool, v_pool):
        for i in range(n_blk):
            args.append(pool)
            in_specs.append(pl.BlockSpec((None, n_kv, dh, psz),
                                         lambda bi, j, pt, i=i: (pt[bi, j * n_blk + i], 0, 0, 0)))
    out = pl.pallas_call(
        functools.partial(_decode_keylane_kernel, n_blk=n_blk),
        out_shape=jax.ShapeDtypeStruct((b, n_kv, rep, dh, 1), F32),
        grid_spec=pltpu.PrefetchScalarGridSpec(
            num_scalar_prefetch=1, grid=(b, n_pages // n_blk), in_specs=in_specs,
            out_specs=pl.BlockSpec((1, n_kv, rep, dh, 1), lambda bi, j, pt: (bi, 0, 0, 0, 0)),
            scratch_shapes=[pltpu.VMEM((n_kv, rep, 1, 1), F32), pltpu.VMEM((n_kv, rep, 1, 1), F32),
                            pltpu.VMEM((n_kv, rep, dh, psz), F32)]),
        compiler_params=_ARB(2), name="decode_attention_keylane",
    )(page_table, *args)
    return out.reshape(b, n_kv, rep, dh)


def _decode_diff_kernel(pt_ref, q_ref, ks_ref, vs_ref, e_ref, *rest, n_blk, n_heads):
    k_refs, v_refs = rest[:n_blk], rest[n_blk:2 * n_blk]
    o_ref, m_sc, l_sc, acc_sc = rest[2 * n_blk:]
    j = pl.program_id(1)

    @pl.when(j == 0)
    def _():
        m_sc[...] = jnp.full(m_sc.shape, NEG, F32)
        l_sc[...] = jnp.zeros(l_sc.shape, F32)
        acc_sc[...] = jnp.zeros(acc_sc.shape, F32)

    q = q_ref[0]
    qb = jnp.broadcast_to(q, q.shape[:2] + (LANES,))
    n_sub = q.shape[0]
    wide = e_ref.shape[1]
    own = (lax.broadcasted_iota(jnp.int32, (n_sub, wide), 1) % n_heads
           == lax.broadcasted_iota(jnp.int32, (n_sub, wide), 0) // (n_sub // n_heads))
    m, l, acc = m_sc[...], l_sc[...], acc_sc[...]
    for i in range(n_blk):
        s = jnp.sum(qb * k_refs[i][...], axis=1)
        m_new = jnp.maximum(m, jnp.max(s, axis=-1, keepdims=True))
        alpha = jnp.exp(m - m_new)
        p = jnp.exp(s - m_new)
        l = alpha * l + jnp.sum(p, axis=-1, keepdims=True)
        spread = jnp.dot(p.astype(BF16), e_ref[...], preferred_element_type=F32)
        spread = jnp.where(own, spread, 0.0).astype(BF16)
        acc = alpha * acc + jnp.dot(spread, v_refs[i][...].astype(BF16), preferred_element_type=F32)
        m = m_new
    m_sc[...], l_sc[...], acc_sc[...] = m, l, acc

    @pl.when(j == pl.num_programs(1) - 1)
    def _():
        m, l, acc = m_sc[...], l_sc[...], acc_sc[...]
        s = jnp.sum(q * ks_ref[0], axis=1)
        m_new = jnp.maximum(m, s)
        alpha = jnp.exp(m - m_new)
        p = jnp.exp(s - m_new)
        o_ref[0] = (alpha * acc + p * vs_ref[0]) / jnp.maximum(alpha * l + p, 1e-30)


def decode_attention_diff(q, k_pool, v_pool, k_self, v_self, page_table):
    b, n_sub, dh = q.shape
    n_pages = page_table.shape[1]
    psz, n_heads, dv = v_pool.shape[1:]
    n_blk = PAGES_PER_STEP
    v_rows = v_pool.reshape(v_pool.shape[0], psz * n_heads, dv)
    expand = jnp.asarray(np.arange(psz)[:, None] == np.arange(psz * n_heads)[None, :] // n_heads, dtype=BF16)
    args = [q.reshape(b, n_sub, dh, 1), k_self.reshape(b, n_sub, dh, 1), v_self, expand]
    in_specs = [pl.BlockSpec((1, n_sub, dh, 1), lambda bi, j, pt: (bi, 0, 0, 0)),
                pl.BlockSpec((1, n_sub, dh, 1), lambda bi, j, pt: (bi, 0, 0, 0)),
                pl.BlockSpec((1, n_sub, dv), lambda bi, j, pt: (bi, 0, 0)),
                pl.BlockSpec(expand.shape, lambda bi, j, pt: (0, 0))]
    for i in range(n_blk):
        args.append(k_pool)
        in_specs.append(pl.BlockSpec((None, n_sub, dh, psz), lambda bi, j, pt, i=i: (pt[bi, j * n_blk + i], 0, 0, 0)))
    for i in range(n_blk):
        args.append(v_rows)
        in_specs.append(pl.BlockSpec((None, psz * n_heads, dv), lambda bi, j, pt, i=i: (pt[bi, j * n_blk + i], 0, 0)))
    return pl.pallas_call(
        functools.partial(_decode_diff_kernel, n_blk=n_blk, n_heads=n_heads),
        out_shape=jax.ShapeDtypeStruct((b, n_sub, dv), F32),
        grid_spec=pltpu.PrefetchScalarGridSpec(
            num_scalar_prefetch=1, grid=(b, n_pages // n_blk), in_specs=in_specs,
            out_specs=pl.BlockSpec((1, n_sub, dv), lambda bi, j, pt: (bi, 0, 0)),
            scratch_shapes=[pltpu.VMEM((n_sub, 1), F32), pltpu.VMEM((n_sub, 1), F32),
                            pltpu.VMEM((n_sub, dv), F32)]),
        compiler_params=_ARB(2), name="decode_attention_diff",
    )(page_table, *args)


def _keys_on_lanes(pool):
    return pool.transpose(0, 2, 3, 1)


def _spread_heads(q, width):
    b = q.shape[0]
    n_kv = width // HEAD_DIM
    owner = (np.arange(DEC_HEADS) * n_kv) // DEC_HEADS
    onehot = jnp.asarray(owner[:, None] == np.arange(n_kv)[None, :], dtype=F32)
    out = q[:, :, None, :] * onehot[None, :, :, None]
    return (out * SCALE).reshape(b, DEC_HEADS, width).astype(BF16)


def _own_lanes(o, width_per_head):
    b, nh, lv = o.shape
    n_kv = lv // width_per_head
    owner = (np.arange(nh) * n_kv) // nh
    o = o.reshape(b, nh, n_kv, width_per_head)
    return jnp.take_along_axis(o, jnp.asarray(owner).reshape(1, nh, 1, 1), axis=2)[:, :, 0]


def _diff_finish_kernel(o0_ref, o1_ref, lam_ref, subln_ref, o_ref, *, lam_init):
    o = o0_ref[...] - lam_ref[...] * o1_ref[...]
    o_ref[...] = _rms(o, subln_ref[...]) * (1.0 - lam_init)


def diff_finish(o0, o1, lam, lam_init, subln):
    m, w = o0.shape
    lam_vec = jnp.broadcast_to(lam.astype(F32).reshape(1, 1), (1, w))
    full = lambda shape: pl.BlockSpec(shape, lambda i: (0, 0))
    return pl.pallas_call(
        functools.partial(_diff_finish_kernel, lam_init=lam_init),
        grid=(1,), in_specs=[full((m, w)), full((m, w)), full((1, w)), full((1, w))],
        out_specs=full((m, w)), out_shape=jax.ShapeDtypeStruct((m, w), F32),
        compiler_params=_ARB(1), name="diff_finish",
    )(o0, o1, lam_vec, subln.reshape(1, w))


def _nsa_cmp_decode_kernel(*refs, n_blk, ncp, ns, past):
    it = iter(refs)
    next(it)
    qt_ref = next(it)
    pages = [[[next(it) for _ in range(n_blk)] for _ in range(2)] for _ in range(2)]
    pek_ref, w1k_ref, w2k_ref, pev_ref, w1v_ref, w2v_ref, ov_ref = (next(it) for _ in range(7))
    o_ref, sel_ref = next(it), next(it)
    ph_refs = [next(it), next(it)]
    j = pl.program_id(1)
    for t in range(2):
        for half in range(2):
            for i in range(n_blk):
                c0 = pl.multiple_of((j * n_blk + i) * 8, 8)
                for l in range(NSA_CMP_STRIDE):
                    ph_refs[t][l, pl.ds(c0, 8), half * LANES:(half + 1) * LANES] = \
                        pages[t][half][i][pl.ds(l, 8, stride=NSA_CMP_STRIDE), :]

    @pl.when(j == pl.num_programs(1) - 1)
    def _():
        kc = _compress_core(lambda l: ph_refs[0][l], pek_ref, w1k_ref, w2k_ref, ncp)
        vc = _compress_core(lambda l: ph_refs[1][l], pev_ref, w1v_ref, w2v_ref, ncp)
        qt = qt_ref[0]
        s = lax.dot_general(qt, kc.astype(BF16), (((1,), (1,)), ((), ())), preferred_element_type=F32)
        valid = lax.broadcasted_iota(jnp.int32, s.shape, 1) < ncp - 1
        s = jnp.where(valid, s, NEG)
        e = jnp.where(valid, jnp.exp(s - jnp.max(s, axis=-1, keepdims=True)), 0.0)
        p = e / jnp.maximum(jnp.sum(e, axis=-1, keepdims=True), 1e-30)
        o_ref[0] = jnp.dot(p.astype(BF16), vc.astype(BF16), preferred_element_type=F32)
        grp = (lax.broadcasted_iota(jnp.int32, (8, DEC_HEADS), 1) // NSA_REP
               == lax.broadcasted_iota(jnp.int32, (8, DEC_HEADS), 0)).astype(F32)
        psum = jnp.dot(grp, p, preferred_element_type=F32, precision=lax.Precision.HIGHEST)
        imp = jnp.dot(psum, ov_ref[...], preferred_element_type=F32, precision=lax.Precision.HIGHEST)
        sel = _select_blocks(imp, jnp.full((8, 1), past, jnp.int32), ns)
        sel_ref[0] = jnp.where(sel, 1.0, 0.0)


def nsa_cmp_decode(qt, pool_k, pool_v, page_table, cw_k, cw_v):
    b = qt.shape[0]
    n_pages = page_table.shape[1]
    past = n_pages * pool_k.shape[1]
    n_blk = PAGES_PER_STEP
    ncp = past // NSA_CMP_STRIDE
    ns = past // NSA_SEL_LEN + 1
    nsp = -(-ns // LANES) * LANES
    ov = jnp.asarray(np.pad(nsa_overlap(ncp, ns), ((0, 0), (0, nsp - ns))))
    args = [qt]
    in_specs = [pl.BlockSpec((1, DEC_HEADS, 4 * HEAD_DIM), lambda bi, j, pt: (bi, 0, 0))]
    for pool in (pool_k, pool_v):
        for half in range(2):
            for i in range(n_blk):
                args.append(pool)
                in_specs.append(pl.BlockSpec((None, pool.shape[1], LANES),
                                             lambda bi, j, pt, i=i, half=half: (pt[bi, j * n_blk + i], 0, half)))
    for a in tuple(cw_k) + tuple(cw_v) + (ov,):
        args.append(a)
        in_specs.append(pl.BlockSpec(a.shape, lambda bi, j, pt, nd=a.ndim: (0,) * nd))
    return pl.pallas_call(
        functools.partial(_nsa_cmp_decode_kernel, n_blk=n_blk, ncp=ncp, ns=ns, past=past),
        out_shape=[jax.ShapeDtypeStruct((b, DEC_HEADS, 4 * HEAD_DIM), F32),
                   jax.ShapeDtypeStruct((b, 8, nsp), F32)],
        grid_spec=pltpu.PrefetchScalarGridSpec(
            num_scalar_prefetch=1, grid=(b, n_pages // n_blk), in_specs=in_specs,
            out_specs=[pl.BlockSpec((1, DEC_HEADS, 4 * HEAD_DIM), lambda bi, j, pt: (bi, 0, 0)),
                       pl.BlockSpec((1, 8, nsp), lambda bi, j, pt: (bi, 0, 0))],
            scratch_shapes=[pltpu.VMEM((NSA_CMP_STRIDE, ncp, 4 * HEAD_DIM), F32)] * 2),
        compiler_params=_ARB(2), name="nsa_cmp_decode",
    )(page_table, *args)


def gather_pages(pool, page_table):
    rows = pool[page_table]
    return rows.reshape((page_table.shape[0], page_table.shape[1] * pool.shape[1]) + pool.shape[2:])


def _pad_cols(w, n):
    return jnp.pad(w, ((0, 0), (0, n - w.shape[1])))


def _pad_vec(v, n):
    return jnp.pad(v, (0, n - v.shape[0]))


def fox_mixer(hp, hs, g, cache_k, cache_v, cache_logf, page_table, w_in, b_f, w_o):
    bp, sp, d = hp.shape
    bs, ds, _ = hs.shape
    past = page_table.shape[1] * cache_k.shape[1]
    hd = FOX_HEADS * HEAD_DIM
    wb = w_in.astype(BF16)
    ws = [wb[:, :hd], wb[:, hd:2 * hd], wb[:, 2 * hd:3 * hd], _pad_cols(wb[:, 3 * hd:], LANES)]
    biases = [None, None, None, _pad_vec(b_f, LANES)]
    wo = w_o.astype(BF16)
    hp2 = hp.reshape(bp * sp, d)
    hs2 = hs.reshape(bs * ds, d)
    cfgs = [ProjCfg(f32=False, bf16=True, scale=SCALE), ProjCfg(bf16=True), ProjCfg(bf16=True),
            ProjCfg(act="logsig")]
    q16, k, k16, v, v16, lf = norm_proj(hp2, g, ws, cfgs, biases)
    logf = lf[:, :FOX_HEADS].reshape(bp, sp, FOX_HEADS)
    c = jnp.cumsum(logf, axis=1)
    o_p = fox_prompt_attention(q16.reshape(bp, sp, hd), k16.reshape(bp, sp, hd), v16.reshape(bp, sp, hd), c)
    hp_new = out_proj([o_p.reshape(bp * sp, hd)], wo, hp2).reshape(bp, sp, d)
    sh = (bp, sp, FOX_HEADS, HEAD_DIM)
    k, v = k.reshape(sh), v.reshape(sh)
    cfgs = [ProjCfg(), ProjCfg(), ProjCfg(), ProjCfg(act="logsig")]
    qs, ks, vs, lfs = norm_proj(hs2, g, ws, cfgs, biases)
    logfs = lfs[:, :FOX_HEADS].reshape(bs, ds, FOX_HEADS)
    c_all = jnp.cumsum(jnp.concatenate([gather_pages(cache_logf, page_table).astype(F32), logfs], axis=1), axis=1)
    bias = (c_all[:, past:] - c_all[:, :past]).transpose(0, 2, 1)
    hsh = (bs, FOX_HEADS, HEAD_DIM)
    o_s = decode_attention_keylane((qs * SCALE).reshape(bs, FOX_HEADS, 1, HEAD_DIM), _keys_on_lanes(cache_k),
                                   _keys_on_lanes(cache_v), bias, ks.reshape(hsh), vs.reshape(hsh), page_table)
    hs_new = out_proj([o_s.reshape(bs * ds, D_MODEL)], wo, hs2).reshape(bs, ds, d)
    shs = (bs, ds, FOX_HEADS, HEAD_DIM)
    ks, vs = ks.reshape(shs), vs.reshape(shs)
    return hp_new, hs_new, (k, v, logf, ks, vs, logfs)


def mlp_both(hp, hs, g, wu, wd):
    wu = wu.astype(BF16)
    wd = wd.astype(BF16)
    yp = mlp(hp.reshape(-1, D_MODEL), g, wu, wd).reshape(hp.shape)
    ys = mlp(hs.reshape(-1, D_MODEL), g, wu, wd).reshape(hs.shape)
    return yp, ys


def nsa_mixer(hp, hs, gn, cache_cmp_k, cache_cmp_v, cache_sel_k, cache_sel_v, state_win_k, state_win_v,
              page_table, w_in, b_gate, pe_k, w1_k, w2_k, pe_v, w1_v, w2_v, w_o):
    bp, sp, d = hp.shape
    bs, ds, _ = hs.shape
    past = page_table.shape[1] * cache_cmp_k.shape[1]
    wb_ = state_win_k.shape[1]
    hq = NSA_HEADS * HEAD_DIM
    kvd = NSA_GROUPS * HEAD_DIM
    wbf = w_in.astype(BF16)
    ws = [wbf[:, :hq]] + [wbf[:, hq + i * kvd:hq + (i + 1) * kvd] for i in range(6)] \
        + [_pad_cols(wbf[:, hq + 6 * kvd:], LANES)]
    biases = [None] * 7 + [_pad_vec(b_gate, LANES)]
    wo = w_o.astype(BF16)
    hp2 = hp.reshape(bp * sp, d)
    hs2 = hs.reshape(bs * ds, d)
    cfgs = [ProjCfg(rope=True, f32=False, bf16=True, scale=SCALE),
            ProjCfg(rope=True), ProjCfg(),
            ProjCfg(rope=True, bf16=True), ProjCfg(bf16=True),
            ProjCfg(rope=True, bf16=True), ProjCfg(bf16=True),
            ProjCfg(act="sigmoid")]
    tabs = rope_tables(jnp.arange(sp, dtype=jnp.int32))
    (q16, kc, vc, ksl, ksl16, vsl, vsl16, kw, kw16, vw, vw16, gt) = norm_proj(hp2, gn, ws, cfgs, biases, tabs)
    r3 = lambda a: a.reshape(bp, sp, a.shape[-1])
    q16 = r3(q16)
    gates = gt[:, :3 * NSA_HEADS].reshape(bp, sp, 3, NSA_HEADS)
    g_cmp = gates[:, :, 0].reshape(bp, sp, NSA_GROUPS, NSA_REP).transpose(0, 2, 1, 3)
    g_sel = gates[:, :, 1].reshape(bp, sp, 2, 8).transpose(0, 2, 3, 1)
    g_win = gates[:, :, 2].reshape(bp, sp, 2, 8).transpose(0, 2, 3, 1)
    kcc = nsa_compress_prompt(r3(kc), nsa_compress_weights(pe_k, w1_k, w2_k))
    vcc = nsa_compress_prompt(r3(vc), nsa_compress_weights(pe_v, w1_v, w2_v))
    o_cmp, sel = nsa_cmp_prompt(q16, kcc, vcc, g_cmp)
    o_sel = nsa_branch_prompt(q16, r3(ksl16), r3(vsl16), g_sel, sel)
    o_win = nsa_branch_prompt(q16, r3(kw16), r3(vw16), g_win)
    hp_new = out_proj([o_cmp.reshape(bp * sp, d), o_sel.reshape(bp * sp, d), o_win.reshape(bp * sp, d)],
                      wo, hp2).reshape(bp, sp, d)
    r4 = lambda a: a.reshape(bp, sp, NSA_GROUPS, HEAD_DIM)
    kc, vc, ksl, vsl, kw, vw = r4(kc), r4(vc), r4(ksl), r4(vsl), r4(kw), r4(vw)
    cfgs = [ProjCfg(rope=True), ProjCfg(rope=True), ProjCfg(), ProjCfg(rope=True), ProjCfg(),
            ProjCfg(rope=True), ProjCfg(), ProjCfg(act="sigmoid")]
    tabs_s = rope_tables(jnp.full((bs * ds,), past, jnp.int32))
    qs, kc_s, vc_s, ksl_s, vsl_s, kw_s, vw_s, gt_s = norm_proj(hs2, gn, ws, cfgs, biases, tabs_s)
    r4s = lambda a: a.reshape(bs, ds, NSA_GROUPS, HEAD_DIM)
    kc_s, vc_s, ksl_s, vsl_s, kw_s, vw_s = r4s(kc_s), r4s(vc_s), r4s(ksl_s), r4s(vsl_s), r4s(kw_s), r4s(vw_s)
    gates_s = gt_s[:, :3 * NSA_HEADS].reshape(bs, ds, 3, NSA_HEADS)
    kvd4 = NSA_GROUPS * HEAD_DIM
    pool = lambda a: a.reshape(a.shape[0], a.shape[1], kvd4)
    row = lambda a: a.reshape(bs, 1, kvd4)
    qt = _spread_heads(qs.reshape(bs, NSA_HEADS, HEAD_DIM), kvd4)
    o_cmp_full, selm = nsa_cmp_decode(qt, pool(cache_cmp_k), pool(cache_cmp_v), page_table,
                                      nsa_compress_weights(pe_k, w1_k, w2_k), nsa_compress_weights(pe_v, w1_v, w2_v))
    n_past_blk = past // NSA_SEL_LEN
    sel_keys = jnp.repeat(selm[:, :NSA_GROUPS, :n_past_blk], NSA_SEL_LEN, axis=2)
    bias_sel = jnp.where(sel_keys > 0.5, 0.0, NEG)
    gsh = (bs, NSA_GROUPS, HEAD_DIM)
    o_sel = decode_attention_keylane((qs * SCALE).reshape(bs, NSA_GROUPS, NSA_REP, HEAD_DIM), _keys_on_lanes(cache_sel_k),
                                     _keys_on_lanes(cache_sel_v), bias_sel, ksl_s.reshape(gsh), vsl_s.reshape(gsh),
                                     page_table).reshape(bs, NSA_HEADS, HEAD_DIM)
    in_win = jnp.arange(wb_) > wb_ - NSA_WINDOW
    bias_win = jnp.broadcast_to(jnp.where(in_win, 0.0, NEG)[None, None, :], (bs, NSA_HEADS, wb_)).astype(F32)
    o_win_full = decode_attention(qt, state_win_k.reshape(bs, wb_, kvd4), state_win_v.reshape(bs, wb_, kvd4),
                                  bias_win, row(kw_s), row(vw_s))
    gs = gates_s.reshape(bs, 3, NSA_HEADS, 1)
    o_s = (gs[:, 0] * _own_lanes(o_cmp_full, HEAD_DIM) + gs[:, 1] * o_sel
           + gs[:, 2] * _own_lanes(o_win_full, HEAD_DIM))
    hs_new = out_proj([o_s.reshape(bs * ds, d)], wo, hs2).reshape(bs, ds, d)
    kw_all = jnp.concatenate([state_win_k, kw_s], axis=1)
    vw_all = jnp.concatenate([state_win_v, vw_s], axis=1)
    keep = min(NSA_WINDOW, sp)
    return hp_new, hs_new, (kc, vc, ksl, vsl, kw[:, sp - keep:], vw[:, sp - keep:],
                            kc_s, vc_s, ksl_s, vsl_s, kw_all[:, ds:], vw_all[:, ds:])


def diff_mixer(hp, hs, gn, cache_k, cache_v, page_table, w_in, lq1, lk1, lq2, lk2, subln, w_o, layer_idx):
    lam_init = 0.8 - 0.6 * math.exp(-0.3 * layer_idx)
    lam = (jnp.exp(jnp.sum(lq1.astype(F32) * lk1.astype(F32)))
           - jnp.exp(jnp.sum(lq2.astype(F32) * lk2.astype(F32))) + lam_init)
    bp, sp, d = hp.shape
    bs, ds, _ = hs.shape
    past = page_table.shape[1] * cache_k.shape[1]
    wbf = w_in.astype(BF16)
    ws = [wbf[:, :d], wbf[:, d:2 * d], wbf[:, 2 * d:]]
    wo = w_o.astype(BF16)
    hp2 = hp.reshape(bp * sp, d)
    hs2 = hs.reshape(bs * ds, d)
    cfgs = [ProjCfg(rope=True, f32=False, bf16=True, scale=SCALE), ProjCfg(rope=True, bf16=True),
            ProjCfg(bf16=True)]
    tabs = rope_tables(jnp.arange(sp, dtype=jnp.int32))
    q16, k, k16, v, v16 = norm_proj(hp2, gn, ws, cfgs, None, tabs)
    r3 = lambda a: a.reshape(bp, sp, d)
    o_p = diff_prompt_attention(r3(q16), r3(k16), r3(v16), lam, lam_init, subln)
    hp_new = out_proj([o_p.reshape(bp * sp, d)], wo, hp2).reshape(bp, sp, d)
    k = k.reshape(bp, sp, 2 * DIFF_HEADS, HEAD_DIM)
    v = v.reshape(bp, sp, DIFF_HEADS, 2 * HEAD_DIM)
    cfgs = [ProjCfg(rope=True), ProjCfg(rope=True), ProjCfg()]
    tabs_s = rope_tables(jnp.full((bs * ds,), past, jnp.int32))
    qs, ks, vs = norm_proj(hs2, gn, ws, cfgs, None, tabs_s)
    o_sub = decode_attention_diff((qs * SCALE).reshape(bs, 2 * DIFF_HEADS, HEAD_DIM), _keys_on_lanes(cache_k), cache_v,
                                  ks.reshape(bs, 2 * DIFF_HEADS, HEAD_DIM),
                                  jnp.repeat(vs.reshape(bs, DIFF_HEADS, 2 * HEAD_DIM), 2, axis=1), page_table)
    o_s = diff_finish(o_sub[:, 0::2].reshape(bs * DIFF_HEADS, 2 * HEAD_DIM),
                      o_sub[:, 1::2].reshape(bs * DIFF_HEADS, 2 * HEAD_DIM), lam, lam_init, subln)
    hs_new = out_proj([o_s.reshape(bs * ds, D_MODEL)], wo, hs2).reshape(bs, ds, d)
    ks = ks.reshape(bs, ds, 2 * DIFF_HEADS, HEAD_DIM)
    vs = vs.reshape(bs, ds, DIFF_HEADS, 2 * HEAD_DIM)
    return hp_new, hs_new, (k, v, ks, vs)


def kernel(x_prompt, x_sample, cache_l0_k, cache_l0_v, cache_l0_logf, cache_l1_cmp_k, cache_l1_cmp_v, cache_l1_sel_k, cache_l1_sel_v, state_l1_win_k, state_l1_win_v, cache_l2_k, cache_l2_v, cache_l3_k, cache_l3_v, cache_l3_logf, page_table, l0_norm_mix, l0_fox_w_in, l0_fox_b_f, l0_fox_w_o, l0_norm_mlp, l0_mlp_up, l0_mlp_down, l1_norm_mix, l1_nsa_w_in, l1_nsa_b_gate, l1_nsa_pe_k, l1_nsa_w1_k, l1_nsa_w2_k, l1_nsa_pe_v, l1_nsa_w1_v, l1_nsa_w2_v, l1_nsa_w_o, l1_norm_mlp, l1_mlp_up, l1_mlp_down, l2_norm_mix, l2_diff_w_in, l2_diff_lq1, l2_diff_lk1, l2_diff_lq2, l2_diff_lk2, l2_diff_subln, l2_diff_w_o, l2_norm_mlp, l2_mlp_up, l2_mlp_down, l3_norm_mix, l3_fox_w_in, l3_fox_b_f, l3_fox_w_o, l3_norm_mlp, l3_mlp_up, l3_mlp_down, norm_final):
    hp, hs = x_prompt, x_sample
    state = []
    hp, hs, st = fox_mixer(hp, hs, l0_norm_mix, cache_l0_k, cache_l0_v, cache_l0_logf, page_table,
                           l0_fox_w_in, l0_fox_b_f, l0_fox_w_o)
    state += st
    hp, hs = mlp_both(hp, hs, l0_norm_mlp, l0_mlp_up, l0_mlp_down)
    hp, hs, st = nsa_mixer(hp, hs, l1_norm_mix, cache_l1_cmp_k, cache_l1_cmp_v, cache_l1_sel_k, cache_l1_sel_v,
                           state_l1_win_k, state_l1_win_v, page_table,
                           l1_nsa_w_in, l1_nsa_b_gate, l1_nsa_pe_k, l1_nsa_w1_k, l1_nsa_w2_k,
                           l1_nsa_pe_v, l1_nsa_w1_v, l1_nsa_w2_v, l1_nsa_w_o)
    state += st
    hp, hs = mlp_both(hp, hs, l1_norm_mlp, l1_mlp_up, l1_mlp_down)
    hp, hs, st = diff_mixer(hp, hs, l2_norm_mix, cache_l2_k, cache_l2_v, page_table, l2_diff_w_in,
                            l2_diff_lq1, l2_diff_lk1, l2_diff_lq2, l2_diff_lk2,
                            l2_diff_subln, l2_diff_w_o, DIFF_LAYER)
    state += st
    hp, hs = mlp_both(hp, hs, l2_norm_mlp, l2_mlp_up, l2_mlp_down)
    hp, hs, st = fox_mixer(hp, hs, l3_norm_mix, cache_l3_k, cache_l3_v, cache_l3_logf, page_table,
                           l3_fox_w_in, l3_fox_b_f, l3_fox_w_o)
    state += st
    hp, hs = mlp_both(hp, hs, l3_norm_mlp, l3_mlp_up, l3_mlp_down)
    y_p = final_norm(hp.reshape(-1, D_MODEL), norm_final).reshape(hp.shape)
    y_s = final_norm(hs.reshape(-1, D_MODEL), norm_final).reshape(hs.shape)
    return (y_p, y_s) + tuple(state)
```

```python
import functools
import math
from typing import NamedTuple

import numpy as np
import jax
import jax.numpy as jnp
from jax import lax
from jax.experimental import pallas as pl
from jax.experimental.pallas import tpu as pltpu

F32 = jnp.float32
BF16 = jnp.bfloat16

D_MODEL = 1024
HEAD_DIM = 64
FOX_HEADS = 16
NSA_HEADS = 16
NSA_GROUPS = 4
NSA_REP = 4
NSA_CMP_LEN = 32
NSA_CMP_STRIDE = 16
NSA_SEL_LEN = 64
NSA_TOP_N = 16
NSA_N_LOCAL = 2
NSA_WINDOW = 512
DIFF_HEADS = 8
D_FF = 4096
ROPE_THETA = 10000.0
NORM_EPS = 1e-6
DIFF_LAYER = 2
SCALE = HEAD_DIM ** -0.5

LANES = 128
VMEM_LIMIT = 56 * 1024 * 1024
ROW_TILE = 512
NEG = -1e30

_ARB = lambda n: pltpu.CompilerParams(dimension_semantics=("arbitrary",) * n,
                                      vmem_limit_bytes=VMEM_LIMIT)


def _row_tile(m):
    return ROW_TILE if m % ROW_TILE == 0 else m


def _rms(x, g):
    return x * lax.rsqrt(jnp.mean(x * x, axis=-1, keepdims=True) + NORM_EPS) * g


def _swap_halves(x):
    parts = []
    for c in range(x.shape[-1] // LANES):
        parts += [x[:, c * LANES + 64:(c + 1) * LANES], x[:, c * LANES:c * LANES + 64]]
    return jnp.concatenate(parts, axis=-1)


class ProjCfg(NamedTuple):
    rope: bool = False
    f32: bool = True
    bf16: bool = False
    scale: float = 1.0
    act: str = ""


def _rope(y, cos, sin):
    n = y.shape[-1]
    reps = n // LANES
    if reps > 1:
        cos = jnp.concatenate([cos] * reps, axis=-1)
        sin = jnp.concatenate([sin] * reps, axis=-1)
    lane = lax.broadcasted_iota(jnp.int32, y.shape, 1)
    partner = jnp.where((lane % HEAD_DIM) < HEAD_DIM // 2,
                        pltpu.roll(y, n - HEAD_DIM // 2, 1), pltpu.roll(y, HEAD_DIM // 2, 1))
    return y * cos + partner * sin


def _norm_proj_kernel(*refs, cfgs, use_rope):
    it = iter(refs)
    x_ref, g_ref = next(it), next(it)
    if use_rope:
        cos_ref, sin_ref = next(it), next(it)
    ins = []
    for c in cfgs:
        w_ref = next(it)
        ins.append((w_ref, next(it) if c.act else None))
    xn = _rms(x_ref[...], g_ref[...]).astype(BF16)
    for c, (w_ref, b_ref) in zip(cfgs, ins):
        y = jnp.dot(xn, w_ref[...], preferred_element_type=F32)
        if c.rope:
            y = _rope(y, cos_ref[...], sin_ref[...])
        if c.act:
            z = y + b_ref[...]
            if c.act == "logsig":
                y = jnp.minimum(z, 0.0) - jnp.log1p(jnp.exp(-jnp.abs(z)))
            else:
                y = 1.0 / (1.0 + jnp.exp(-z))
        if c.f32:
            next(it)[...] = y
        if c.bf16:
            next(it)[...] = (y * c.scale).astype(BF16)


def norm_proj(x, g, ws, cfgs, biases=None, rope_tables=None):
    m, d = x.shape
    tm = _row_tile(m)
    use_rope = rope_tables is not None
    args = [x, g.reshape(1, d)]
    in_specs = [pl.BlockSpec((tm, d), lambda i: (i, 0)), pl.BlockSpec((1, d), lambda i: (0, 0))]
    if use_rope:
        nt = rope_tables[0].shape[0] // tm
        for t in rope_tables:
            args.append(t)
            in_specs.append(pl.BlockSpec((tm, LANES), lambda i: (i % nt, 0)))
    out_specs, out_shape = [], []
    for ci, (w, c) in enumerate(zip(ws, cfgs)):
        n = w.shape[1]
        args.append(w)
        in_specs.append(pl.BlockSpec(w.shape, lambda i: (0, 0)))
        if c.act:
            args.append(biases[ci].reshape(1, n))
            in_specs.append(pl.BlockSpec((1, n), lambda i: (0, 0)))
        for flag, dt in ((c.f32, F32), (c.bf16, BF16)):
            if flag:
                out_specs.append(pl.BlockSpec((tm, n), lambda i: (i, 0)))
                out_shape.append(jax.ShapeDtypeStruct((m, n), dt))
    return pl.pallas_call(
        functools.partial(_norm_proj_kernel, cfgs=tuple(cfgs), use_rope=use_rope),
        grid=(m // tm,), in_specs=in_specs, out_specs=out_specs, out_shape=out_shape,
        compiler_params=_ARB(1), name="norm_proj",
    )(*args)


def _out_proj_kernel(*refs, n_in):
    a_refs = refs[:n_in]
    w_ref, res_ref, o_ref = refs[n_in:]
    a = a_refs[0][...]
    for r in a_refs[1:]:
        a = a + r[...]
    o_ref[...] = res_ref[...] + jnp.dot(a.astype(BF16), w_ref[...], preferred_element_type=F32)


def out_proj(a_list, w, res):
    m, d = res.shape
    tm = _row_tile(m)
    k = w.shape[0]
    return pl.pallas_call(
        functools.partial(_out_proj_kernel, n_in=len(a_list)),
        grid=(m // tm,),
        in_specs=[pl.BlockSpec((tm, k), lambda i: (i, 0)) for _ in a_list]
                 + [pl.BlockSpec(w.shape, lambda i: (0, 0)),
                    pl.BlockSpec((tm, d), lambda i: (i, 0))],
        out_specs=pl.BlockSpec((tm, d), lambda i: (i, 0)),
        out_shape=jax.ShapeDtypeStruct((m, d), F32),
        compiler_params=_ARB(1), name="out_proj",
    )(*a_list, w, res)


FF_CHUNK = 1024


def _mlp_kernel(x_ref, g_ref, wu_ref, wd_ref, o_ref):
    x = x_ref[...]
    xn = _rms(x, g_ref[...]).astype(BF16)
    acc = x
    for c in range(D_FF // FF_CHUNK):
        h = jnp.dot(xn, wu_ref[:, c * FF_CHUNK:(c + 1) * FF_CHUNK], preferred_element_type=F32)
        h = jnp.maximum(h, 0.0)
        acc = acc + jnp.dot((h * h).astype(BF16), wd_ref[c * FF_CHUNK:(c + 1) * FF_CHUNK, :],
                            preferred_element_type=F32)
    o_ref[...] = acc


def mlp(x, g, wu, wd):
    m, d = x.shape
    tm = _row_tile(m)
    return pl.pallas_call(
        _mlp_kernel,
        grid=(m // tm,),
        in_specs=[pl.BlockSpec((tm, d), lambda i: (i, 0)),
                  pl.BlockSpec((1, d), lambda i: (0, 0)),
                  pl.BlockSpec(wu.shape, lambda i: (0, 0)),
                  pl.BlockSpec(wd.shape, lambda i: (0, 0))],
        out_specs=pl.BlockSpec((tm, d), lambda i: (i, 0)),
        out_shape=jax.ShapeDtypeStruct((m, d), F32),
        compiler_params=_ARB(1), name="mlp",
    )(x, g.reshape(1, d), wu, wd)


def _final_norm_kernel(x_ref, g_ref, o_ref):
    o_ref[...] = _rms(x_ref[...], g_ref[...])


def final_norm(x, g):
    m, d = x.shape
    tm = _row_tile(m)
    return pl.pallas_call(
        _final_norm_kernel,
        grid=(m // tm,),
        in_specs=[pl.BlockSpec((tm, d), lambda i: (i, 0)),
                  pl.BlockSpec((1, d), lambda i: (0, 0))],
        out_specs=pl.BlockSpec((tm, d), lambda i: (i, 0)),
        out_shape=jax.ShapeDtypeStruct((m, d), F32),
        compiler_params=_ARB(1), name="final_norm",
    )(x, g.reshape(1, d))


def rope_tables(pos):
    half = HEAD_DIM // 2
    inv = ROPE_THETA ** (-jnp.arange(half, dtype=F32) / half)
    ang = pos.astype(F32)[:, None] * inv[None, :]
    cos, sin = jnp.cos(ang), jnp.sin(ang)
    cos = jnp.concatenate([cos, cos, cos, cos], axis=-1)
    sin = jnp.concatenate([-sin, sin, -sin, sin], axis=-1)
    return cos, sin


def _flash_streams(qs, load_kv, j_lo, j_diag, mask_fn, robust, row_bias=None, col_bias_fn=None):
    n = len(qs)
    cols = qs[0].shape[0]

    def step(j, carry, diag):
        k, vt = load_kv(j)
        out = []
        for i in range(n):
            m, l, acc = carry[i]
            s = lax.dot_general(k, qs[i], (((1,), (1,)), ((), ())), preferred_element_type=F32)
            if col_bias_fn is not None:
                s = s + col_bias_fn(i, j)
            msk = mask_fn(i, j, diag) if mask_fn is not None else None
            if msk is not None:
                s = jnp.where(msk, s, NEG)
            rmax = jnp.max(s, axis=0, keepdims=True)
            if row_bias is not None:
                rmax = rmax + row_bias[i]
            m_new = jnp.maximum(m, rmax)
            shift = m_new - row_bias[i] if row_bias is not None else m_new
            alpha = jnp.exp(m - m_new)
            p = jnp.exp(s - shift)
            if msk is not None and robust:
                p = jnp.where(msk, p, 0.0)
            l = alpha * l + jnp.sum(p, axis=0, keepdims=True)
            acc = alpha * acc + jnp.dot(vt, p.astype(BF16), preferred_element_type=F32)
            out.append((m_new, l, acc))
        return tuple(out)

    dv = load_kv(0)[1].shape[0]
    one = (jnp.full((1, cols), NEG, F32), jnp.zeros((1, cols), F32), jnp.zeros((dv, cols), F32))
    carry = lax.fori_loop(j_lo, j_diag, lambda j, c: step(j, c, False), (one,) * n)
    return [acc / jnp.maximum(l, 1e-30) for _, l, acc in step(j_diag, carry, True)]


def _causal_mask(q0, tq, cols, j, tk):
    kpos = j * tk + lax.broadcasted_iota(jnp.int32, (tk, cols), 0)
    qpos = q0 + lax.broadcasted_iota(jnp.int32, (tk, cols), 1) % tq
    return qpos, kpos


def _chunked_t(x, tk):
    b, s, w = x.shape
    return x.reshape(b, s // tk, tk, w // LANES, LANES).transpose(0, 3, 1, 4, 2)


def _head_pair_kernel(*refs, tq, tk, mode, lam_init):
    if mode == "fox":
        q_ref, k_ref, vt_ref, cq_ref, ck_ref, o_ref = refs
    else:
        q_ref, k_ref, vt_ref, lam_ref, subln_ref, o_ref = refs
    q0 = pl.program_id(2) * tq
    q = q_ref[0]
    lane = lax.broadcasted_iota(jnp.int32, (tq, LANES), 1)
    zero = jnp.zeros_like(q)

    def load_kv(j):
        st = pl.multiple_of(j * tk, tk)
        return k_ref[0, pl.ds(st, tk), :], vt_ref[0, 0, j]

    def mask_fn(i, j, diag):
        if not diag:
            return None
        qpos, kpos = _causal_mask(q0, tq, tq, j, tk)
        return kpos <= qpos

    qs = [jnp.where((lane >= 64 * i) & (lane < 64 * (i + 1)), q, zero) for i in range(2)]
    if mode == "fox":
        outs = _flash_streams(qs, load_kv, 0, q0 // tk, mask_fn, False,
                              row_bias=[cq_ref[0, 0, i:i + 1, :] for i in range(2)],
                              col_bias_fn=lambda i, j: -ck_ref[0, 0, j][:, i:i + 1])
        sub = lax.broadcasted_iota(jnp.int32, outs[0].shape, 0)
        ot = jnp.where(sub < 64, outs[0], outs[1])
    else:
        outs = _flash_streams(qs, load_kv, 0, q0 // tk, mask_fn, False)
        ot = outs[0] - lam_ref[...] * outs[1]
        ms = jnp.mean(ot * ot, axis=0, keepdims=True)
        ot = ot * lax.rsqrt(ms + NORM_EPS) * subln_ref[...] * (1.0 - lam_init)
    o_ref[0] = ot.T.astype(o_ref.dtype)


ATT_TQ = 512
ATT_TK = 1024
ATT_TQ_NSA = 256


def fox_prompt_attention(q, k, v, c):
    b, s, d = q.shape
    tq, tk = min(ATT_TQ, s), min(ATT_TK, s)
    hp = d // LANES
    cq = c.reshape(b, s, hp, 2).transpose(0, 2, 3, 1)
    ck = c.reshape(b, s // tk, tk, hp, 2).transpose(0, 3, 1, 2, 4)
    return pl.pallas_call(
        functools.partial(_head_pair_kernel, tq=tq, tk=tk, mode="fox", lam_init=0.0),
        grid=(b, hp, s // tq),
        in_specs=[pl.BlockSpec((1, tq, LANES), lambda bi, h, qi: (bi, qi, h)),
                  pl.BlockSpec((1, s, LANES), lambda bi, h, qi: (bi, 0, h)),
                  pl.BlockSpec((1, 1, s // tk, LANES, tk), lambda bi, h, qi: (bi, h, 0, 0, 0)),
                  pl.BlockSpec((1, 1, 2, tq), lambda bi, h, qi: (bi, h, 0, qi)),
                  pl.BlockSpec((1, 1, s // tk, tk, 2), lambda bi, h, qi: (bi, h, 0, 0, 0))],
        out_specs=pl.BlockSpec((1, tq, LANES), lambda bi, h, qi: (bi, qi, h)),
        out_shape=jax.ShapeDtypeStruct((b, s, d), BF16),
        compiler_params=_ARB(3), name="fox_prompt_attention",
    )(q, k, _chunked_t(v, tk), cq, ck)


def diff_prompt_attention(q, k, v, lam, lam_init, subln):
    b, s, d = q.shape
    tq, tk = min(ATT_TQ, s), min(ATT_TK, s)
    hp = d // LANES
    lam_vec = jnp.broadcast_to(lam.astype(F32).reshape(1, 1), (1, tq))
    return pl.pallas_call(
        functools.partial(_head_pair_kernel, tq=tq, tk=tk, mode="diff", lam_init=lam_init),
        grid=(b, hp, s // tq),
        in_specs=[pl.BlockSpec((1, tq, LANES), lambda bi, h, qi: (bi, qi, h)),
                  pl.BlockSpec((1, s, LANES), lambda bi, h, qi: (bi, 0, h)),
                  pl.BlockSpec((1, 1, s // tk, LANES, tk), lambda bi, h, qi: (bi, h, 0, 0, 0)),
                  pl.BlockSpec((1, tq), lambda bi, h, qi: (0, 0)),
                  pl.BlockSpec((LANES, 1), lambda bi, h, qi: (0, 0))],
        out_specs=pl.BlockSpec((1, tq, LANES), lambda bi, h, qi: (bi, qi, h)),
        out_shape=jax.ShapeDtypeStruct((b, s, d), BF16),
        compiler_params=_ARB(3), name="diff_prompt_attention",
    )(q, k, _chunked_t(v, tk), lam_vec, subln.reshape(LANES, 1))


def _compress_core(get_phase, pe_ref, w1_ref, w2_ref, nc):
    lo = jnp.zeros((nc, 4 * HEAD_DIM), F32)
    hi = jnp.zeros((nc, 4 * HEAD_DIM), F32)
    for l in range(NSA_CMP_STRIDE):
        ph = get_phase(l)
        lo = lo + jnp.dot((ph + pe_ref[l:l + 1, :]).astype(BF16), w1_ref[l],
                          preferred_element_type=F32)
        hi = hi + jnp.dot((ph + pe_ref[l + NSA_CMP_STRIDE:l + NSA_CMP_STRIDE + 1, :]).astype(BF16),
                          w1_ref[l + NSA_CMP_STRIDE], preferred_element_type=F32)
    pre = lo + pltpu.roll(hi, nc - 1, 0)
    act = jax.nn.gelu(pre)
    return jnp.dot(act.astype(BF16), w2_ref[...], preferred_element_type=F32)


def _nsa_compress_kernel(rows_ref, pe_ref, w1_ref, w2_ref, o_ref, *, nc):
    out = _compress_core(lambda l: rows_ref[0, l], pe_ref, w1_ref, w2_ref, nc)
    for g in range(NSA_GROUPS):
        o_ref[0, g] = out[:, g * HEAD_DIM:(g + 1) * HEAD_DIM].astype(o_ref.dtype)


def _block_diag4(w):
    eye = jnp.eye(NSA_GROUPS, dtype=w.dtype)
    out = jnp.einsum("gh,...ij->...gihj", eye, w)
    return out.reshape(w.shape[:-2] + (NSA_GROUPS * HEAD_DIM, NSA_GROUPS * HEAD_DIM))


def nsa_compress_weights(pe, w1, w2):
    pe4 = jnp.tile(pe, (1, NSA_GROUPS))
    w1bd = _block_diag4(w1.reshape(NSA_CMP_LEN, HEAD_DIM, HEAD_DIM)).astype(BF16)
    w2bd = _block_diag4(w2).astype(BF16)
    return pe4, w1bd, w2bd


def nsa_compress_prompt(rows, cw):
    b, s, d = rows.shape
    nc = s // NSA_CMP_STRIDE
    pe4, w1bd, w2bd = cw
    return pl.pallas_call(
        functools.partial(_nsa_compress_kernel, nc=nc),
        grid=(b,),
        in_specs=[pl.BlockSpec((1, NSA_CMP_STRIDE, nc, d), lambda bi: (bi, 0, 0, 0)),
                  pl.BlockSpec(pe4.shape, lambda bi: (0, 0)),
                  pl.BlockSpec(w1bd.shape, lambda bi: (0, 0, 0)),
                  pl.BlockSpec(w2bd.shape, lambda bi: (0, 0))],
        out_specs=pl.BlockSpec((1, NSA_GROUPS, nc, HEAD_DIM), lambda bi: (bi, 0, 0, 0)),
        out_shape=jax.ShapeDtypeStruct((b, NSA_GROUPS, nc, HEAD_DIM), BF16),
        compiler_params=_ARB(1), name="nsa_compress",
    )(rows.reshape(b, nc, NSA_CMP_STRIDE, d).transpose(0, 2, 1, 3), pe4, w1bd, w2bd)


def _select_blocks(imp, pos, ns):
    blk = lax.broadcasted_iota(jnp.int32, imp.shape, 1)
    cur = pos // NSA_SEL_LEN
    valid = blk * NSA_SEL_LEN <= pos
    forced = (blk == 0) | ((blk <= cur) & (blk > cur - NSA_N_LOCAL))
    score = jnp.where(forced, jnp.inf, jnp.where(valid, imp, -jnp.inf))
    rank = jnp.zeros(imp.shape, jnp.int32)
    for i in range(ns):
        ci = score[:, i:i + 1]
        ahead = (ci > score) | ((ci == score) & (blk > i))
        rank = rank + ahead.astype(jnp.int32)
    return rank < NSA_TOP_N


def _select_blocks_t(imp, pos, ns):
    blk = lax.broadcasted_iota(jnp.int32, imp.shape, 0)
    cur = pos // NSA_SEL_LEN
    valid = blk * NSA_SEL_LEN <= pos
    forced = (blk == 0) | ((blk <= cur) & (blk > cur - NSA_N_LOCAL))
    score = jnp.where(forced, jnp.inf, jnp.where(valid, imp, -jnp.inf))
    rank = jnp.zeros(imp.shape, jnp.int32)
    for i in range(ns):
        ci = score[i:i + 1, :]
        ahead = (ci > score) | ((ci == score) & (blk > i))
        rank = rank + ahead.astype(jnp.int32)
    return rank < NSA_TOP_N


def _nsa_cmp_kernel(q_ref, kc_ref, vc_ref, ov_ref, gate_ref, o_ref, sel_ref, *, tq, nc, ns):
    q0 = pl.program_id(2) * tq
    q = q_ref[0]
    kc, vc = kc_ref[0, 0], vc_ref[0, 0]
    pos = q0 + lax.broadcasted_iota(jnp.int32, (tq, 1), 0)
    c_last = lax.broadcasted_iota(jnp.int32, (1, nc), 1) * NSA_CMP_STRIDE + (NSA_CMP_LEN - 1)
    valid = c_last <= pos
    gate = gate_ref[0, 0]
    psum = jnp.zeros((tq, nc), F32)
    outs = []
    for r in range(NSA_REP):
        qr = q[:, r * HEAD_DIM:(r + 1) * HEAD_DIM]
        s = lax.dot_general(qr, kc, (((1,), (1,)), ((), ())), preferred_element_type=F32)
        s = jnp.where(valid, s, NEG)
        e = jnp.where(valid, jnp.exp(s - jnp.max(s, axis=-1, keepdims=True)), 0.0)
        p = e / jnp.maximum(jnp.sum(e, axis=-1, keepdims=True), 1e-30)
        outs.append(jnp.dot(p.astype(BF16), vc, preferred_element_type=F32) * gate[:, r:r + 1])
        psum = psum + p
    o_ref[0] = jnp.concatenate(outs, axis=-1)
    imp = lax.dot_general(ov_ref[...], psum, (((1,), (1,)), ((), ())), preferred_element_type=F32,
                          precision=lax.Precision.HIGHEST)
    pos_row = q0 + lax.broadcasted_iota(jnp.int32, (1, tq), 1)
    sel = _select_blocks_t(imp, pos_row, ns)
    sel_ref[0, 0] = jnp.where(sel, 1.0, 0.0).astype(sel_ref.dtype)


def nsa_overlap(nc, ns):
    c0 = np.arange(nc)[:, None] * NSA_CMP_STRIDE
    s0 = np.arange(ns)[None, :] * NSA_SEL_LEN
    ov = np.minimum(c0 + NSA_CMP_LEN, s0 + NSA_SEL_LEN) - np.maximum(c0, s0)
    return (np.clip(ov, 0, None) / NSA_CMP_STRIDE).astype(np.float32)


def nsa_cmp_prompt(q, kc, vc, gate):
    b, s, d = q.shape
    tq = min(ATT_TQ_NSA, s)
    nc, ns = s // NSA_CMP_STRIDE, s // NSA_SEL_LEN
    gw = NSA_REP * HEAD_DIM
    ov = jnp.asarray(nsa_overlap(nc, ns).T)
    return pl.pallas_call(
        functools.partial(_nsa_cmp_kernel, tq=tq, nc=nc, ns=ns),
        grid=(b, NSA_GROUPS, s // tq),
        in_specs=[pl.BlockSpec((1, tq, gw), lambda bi, g, qi: (bi, qi, g)),
                  pl.BlockSpec((1, 1, nc, HEAD_DIM), lambda bi, g, qi: (bi, g, 0, 0)),
                  pl.BlockSpec((1, 1, nc, HEAD_DIM), lambda bi, g, qi: (bi, g, 0, 0)),
                  pl.BlockSpec((ns, nc), lambda bi, g, qi: (0, 0)),
                  pl.BlockSpec((1, 1, tq, NSA_REP), lambda bi, g, qi: (bi, g, qi, 0))],
        out_specs=[pl.BlockSpec((1, tq, gw), lambda bi, g, qi: (bi, qi, g)),
                   pl.BlockSpec((1, 1, ns, tq), lambda bi, g, qi: (bi, g, 0, qi))],
        out_shape=[jax.ShapeDtypeStruct((b, s, d), F32),
                   jax.ShapeDtypeStruct((b, NSA_GROUPS, ns, s), BF16)],
        compiler_params=_ARB(3), name="nsa_cmp_attention",
    )(q, kc, vc, ov, gate)


def _nsa_branch_kernel(*refs, tq, tk, mode):
    if mode == "sel":
        q_ref, k_ref, vt_ref, gate_ref, sel_ref, e_ref, o_ref = refs
    else:
        q_ref, k_ref, vt_ref, gate_ref, o_ref = refs
    q0 = pl.program_id(2) * tq
    q = q_ref[0]
    gate = gate_ref[0, 0]
    lane = lax.broadcasted_iota(jnp.int32, (tq, LANES), 1)
    cols = NSA_REP * tq

    def load_kv(j):
        st = pl.multiple_of(j * tk, tk)
        return k_ref[0, pl.ds(st, tk), :], vt_ref[0, 0, j]

    qs, sms = [], []
    for i in range(2):
        keep = (lane >= 64 * i) & (lane < 64 * (i + 1))
        parts = []
        for r in range(NSA_REP):
            c = 2 * i + r // 2
            x = q[:, c * LANES:(c + 1) * LANES]
            if r % 2 != i:
                x = _swap_halves(x)
            parts.append(jnp.where(keep, x, jnp.zeros_like(x)))
        qs.append(jnp.concatenate(parts, axis=0))
        if mode == "sel":
            penalty = (sel_ref[0, i].astype(F32) - 1.0) * (-NEG)
            sms.append(jnp.concatenate([penalty.astype(BF16)] * NSA_REP, axis=1))
    if mode == "sel":
        def mask_fn(i, j, diag):
            if not diag:
                return None
            qpos, kpos = _causal_mask(q0, tq, cols, j, tk)
            return kpos <= qpos
        bias_fn = lambda i, j: jnp.dot(e_ref[j], sms[i], preferred_element_type=F32)
        j_lo = 0
    else:
        def mask_fn(i, j, diag):
            qpos, kpos = _causal_mask(q0, tq, cols, j, tk)
            return (kpos <= qpos) & (qpos - kpos < NSA_WINDOW)
        bias_fn = None
        j_lo = jnp.maximum(q0 - (NSA_WINDOW - 1), 0) // tk
    outs = _flash_streams(qs, load_kv, j_lo, q0 // tk, mask_fn, True, col_bias_fn=bias_fn)
    pieces = []
    for i in range(2):
        for r in range(NSA_REP):
            hh = NSA_REP * i + r
            pieces.append(outs[i][64 * i:64 * (i + 1), r * tq:(r + 1) * tq] * gate[hh:hh + 1, :])
    o_ref[0] = jnp.concatenate(pieces, axis=0).T


SEL_TK = 512
WIN_TK = 256


def nsa_branch_prompt(q, k, v, gate, sel=None):
    b, s, d = q.shape
    mode = "sel" if sel is not None else "win"
    tq = min(ATT_TQ_NSA, s)
    tk = min(SEL_TK if mode == "sel" else WIN_TK, s)
    ns = s // NSA_SEL_LEN
    qw = 2 * NSA_REP * HEAD_DIM
    args = [q, k, _chunked_t(v, tk), gate]
    in_specs = [pl.BlockSpec((1, tq, qw), lambda bi, gp, qi: (bi, qi, gp)),
                pl.BlockSpec((1, s, LANES), lambda bi, gp, qi: (bi, 0, gp)),
                pl.BlockSpec((1, 1, s // tk, LANES, tk), lambda bi, gp, qi: (bi, gp, 0, 0, 0)),
                pl.BlockSpec((1, 1, 8, tq), lambda bi, gp, qi: (bi, gp, 0, qi))]
    if mode == "sel":
        kb = (np.arange(s) // NSA_SEL_LEN).reshape(s // tk, tk, 1)
        expand = jnp.asarray(kb == np.arange(ns).reshape(1, 1, ns), dtype=BF16)
        args += [sel, expand]
        in_specs += [pl.BlockSpec((1, 2, ns, tq), lambda bi, gp, qi: (bi, gp, 0, qi)),
                     pl.BlockSpec(expand.shape, lambda bi, gp, qi: (0, 0, 0))]
    return pl.pallas_call(
        functools.partial(_nsa_branch_kernel, tq=tq, tk=tk, mode=mode),
        grid=(b, 2, s // tq), in_specs=in_specs,
        out_specs=pl.BlockSpec((1, tq, qw), lambda bi, gp, qi: (bi, qi, gp)),
        out_shape=jax.ShapeDtypeStruct((b, s, d), F32),
        compiler_params=_ARB(3), name="nsa_%s_attention" % mode,
    )(*args)


PAGES_PER_STEP = 8
DEC_HEADS = 16


def _decode_attn_kernel(*refs, n_blk, blk_rows, has_bias, has_self, paged):
    it = iter(refs)
    if paged:
        next(it)
    qt_ref = next(it)
    bias_ref = next(it) if has_bias else None
    if has_self:
        ks_ref, vs_ref = next(it), next(it)
    k_refs = [next(it) for _ in range(n_blk)]
    v_refs = [next(it) for _ in range(n_blk)]
    o_ref = next(it)
    m_sc, l_sc, acc_sc = next(it), next(it), next(it)
    j = pl.program_id(1)

    @pl.when(j == 0)
    def _():
        m_sc[...] = jnp.full(m_sc.shape, NEG, F32)
        l_sc[...] = jnp.zeros(l_sc.shape, F32)
        acc_sc[...] = jnp.zeros(acc_sc.shape, F32)

    qt = qt_ref[0]
    m, l, acc = m_sc[...], l_sc[...], acc_sc[...]
    for i in range(n_blk):
        k = k_refs[i][...].astype(BF16)
        v = v_refs[i][...].astype(BF16)
        s = lax.dot_general(qt, k, (((1,), (1,)), ((), ())), preferred_element_type=F32)
        if has_bias:
            s = s + bias_ref[0, :, i * blk_rows:(i + 1) * blk_rows]
        m_new = jnp.maximum(m, jnp.max(s, axis=-1, keepdims=True))
        alpha = jnp.exp(m - m_new)
        p = jnp.exp(s - m_new)
        l = alpha * l + jnp.sum(p, axis=-1, keepdims=True)
        acc = alpha * acc + jnp.dot(p.astype(BF16), v, preferred_element_type=F32)
        m = m_new
    m_sc[...], l_sc[...], acc_sc[...] = m, l, acc

    @pl.when(j == pl.num_programs(1) - 1)
    def _():
        m, l, acc = m_sc[...], l_sc[...], acc_sc[...]
        if has_self:
            s = jnp.sum(qt.astype(F32) * ks_ref[0], axis=-1, keepdims=True)
            m_new = jnp.maximum(m, s)
            alpha = jnp.exp(m - m_new)
            p = jnp.exp(s - m_new)
            l = alpha * l + p
            acc = alpha * acc + p * vs_ref[0]
        o_ref[0] = acc / jnp.maximum(l, 1e-30)


def decode_attention(qt, k, v, bias=None, k_self=None, v_self=None, page_table=None):
    b, nh, lk = qt.shape
    lv = v.shape[-1]
    paged = page_table is not None
    has_bias, has_self = bias is not None, k_self is not None
    if paged:
        blk_rows, n_blk = k.shape[1], PAGES_PER_STEP
        steps = page_table.shape[1] // n_blk
    else:
        blk_rows, n_blk = min(k.shape[1], 512), 1
        steps = k.shape[1] // blk_rows
    pre = (lambda f: (lambda bi, j, pt: f(bi, j, pt))) if paged else (lambda f: (lambda bi, j: f(bi, j, None)))
    args = [qt]
    in_specs = [pl.BlockSpec((1, nh, lk), pre(lambda bi, j, pt: (bi, 0, 0)))]
    if has_bias:
        args.append(bias)
        in_specs.append(pl.BlockSpec((1, nh, n_blk * blk_rows), pre(lambda bi, j, pt: (bi, 0, j))))
    if has_self:
        args += [k_self, v_self]
        in_specs += [pl.BlockSpec((1, 1, lk), pre(lambda bi, j, pt: (bi, 0, 0))),
                     pl.BlockSpec((1, 1, lv), pre(lambda bi, j, pt: (bi, 0, 0)))]
    for arr, width in ((k, lk), (v, lv)):
        for i in range(n_blk):
            args.append(arr)
            if paged:
                in_specs.append(pl.BlockSpec((None, blk_rows, width),
                                             lambda bi, j, pt, i=i: (pt[bi, j * n_blk + i], 0, 0)))
            else:
                in_specs.append(pl.BlockSpec((None, blk_rows, width), lambda bi, j: (bi, j, 0)))
    out_spec = pl.BlockSpec((1, nh, lv), pre(lambda bi, j, pt: (bi, 0, 0)))
    scratch = [pltpu.VMEM((nh, 1), F32), pltpu.VMEM((nh, 1), F32), pltpu.VMEM((nh, lv), F32)]
    kern = functools.partial(_decode_attn_kernel, n_blk=n_blk, blk_rows=blk_rows,
                             has_bias=has_bias, has_self=has_self, paged=paged)
    out_shape = jax.ShapeDtypeStruct((b, nh, lv), F32)
    if paged:
        return pl.pallas_call(
            kern, out_shape=out_shape,
            grid_spec=pltpu.PrefetchScalarGridSpec(
                num_scalar_prefetch=1, grid=(b, steps), in_specs=in_specs, out_specs=out_spec,
                scratch_shapes=scratch),
            compiler_params=_ARB(2), name="decode_attention_paged",
        )(page_table, *args)
    return pl.pallas_call(
        kern, out_shape=out_shape, grid=(b, steps), in_specs=in_specs, out_specs=out_spec,
        scratch_shapes=scratch, compiler_params=_ARB(2), name="decode_attention",
    )(*args)


def _decode_keylane_kernel(pt_ref, q_ref, bias_ref, ks_ref, vs_ref, *rest, n_blk):
    k_refs, v_refs = rest[:n_blk], rest[n_blk:2 * n_blk]
    o_ref, m_sc, l_sc, acc_sc = rest[2 * n_blk:]
    j = pl.program_id(1)

    @pl.when(j == 0)
    def _():
        m_sc[...] = jnp.full(m_sc.shape, NEG, F32)
        l_sc[...] = jnp.zeros(l_sc.shape, F32)
        acc_sc[...] = jnp.zeros(acc_sc.shape, F32)

    q = q_ref[0]
    qb = jnp.broadcast_to(q, acc_sc.shape)
    m, l, acc = m_sc[...], l_sc[...], acc_sc[...]
    for i in range(n_blk):
        kt = k_refs[i][...][:, None]
        vt = v_refs[i][...][:, None]
        s = jnp.sum(qb * kt, axis=2, keepdims=True) + bias_ref[0, i]
        m_new = jnp.maximum(m, jnp.max(s, axis=-1, keepdims=True))
        alpha = jnp.exp(m - m_new)
        p = jnp.exp(s - m_new)
        l = alpha * l + jnp.sum(p, axis=-1, keepdims=True)
        acc = alpha * acc + p * vt
        m = m_new
    m_sc[...], l_sc[...], acc_sc[...] = m, l, acc

    @pl.when(j == pl.num_programs(1) - 1)
    def _():
        m, l, acc = m_sc[...], l_sc[...], acc_sc[...]
        s = jnp.sum(q * ks_ref[0], axis=2, keepdims=True)
        m_new = jnp.maximum(m, s)
        alpha = jnp.exp(m - m_new)
        p = jnp.exp(s - m_new)
        l = alpha * l + p
        o = alpha * jnp.sum(acc, axis=-1, keepdims=True) + p * vs_ref[0]
        o_ref[0] = o / jnp.maximum(l, 1e-30)


def decode_attention_keylane(q, k_pool, v_pool, bias, k_self, v_self, page_table):
    b, n_kv, rep, dh = q.shape
    n_pages = page_table.shape[1]
    psz = k_pool.shape[-1]
    n_blk = PAGES_PER_STEP
    bias6 = bias.reshape(b, n_kv, n_pages, 1, 1, psz).transpose(0, 2, 1, 3, 4, 5)
    args = [q.reshape(b, n_kv, rep, dh, 1), bias6, k_self.reshape(b, n_kv, 1, dh, 1), v_self.reshape(b, n_kv, 1, dh, 1)]
    in_specs = [pl.BlockSpec((1, n_kv, rep, dh, 1), lambda bi, j, pt: (bi, 0, 0, 0, 0)),
                pl.BlockSpec((1, n_blk, n_kv, 1, 1, psz), lambda bi, j, pt: (bi, j, 0, 0, 0, 0)),
                pl.BlockSpec((1, n_kv, 1, dh, 1), lambda bi, j, pt: (bi, 0, 0, 0, 0)),
                pl.BlockSpec((1, n_kv, 1, dh, 1), lambda bi, j, pt: (bi, 0, 0, 0, 0))]
    for pool in (k_pool, v_pool):
        for i in range(n_blk):
            args.append(pool)
            in_specs.append(pl.BlockSpec((None, n_kv, dh, psz),
                                         lambda bi, j, pt, i=i: (pt[bi, j * n_blk + i], 0, 0, 0)))
    out = pl.pallas_call(
        functools.partial(_decode_keylane_kernel, n_blk=n_blk),
        out_shape=jax.ShapeDtypeStruct((b, n_kv, rep, dh, 1), F32),
        grid_spec=pltpu.PrefetchScalarGridSpec(
            num_scalar_prefetch=1, grid=(b, n_pages // n_blk), in_specs=in_specs,
            out_specs=pl.BlockSpec((1, n_kv, rep, dh, 1), lambda bi, j, pt: (bi, 0, 0, 0, 0)),
            scratch_shapes=[pltpu.VMEM((n_kv, rep, 1, 1), F32), pltpu.VMEM((n_kv, rep, 1, 1), F32),
                            pltpu.VMEM((n_kv, rep, dh, psz), F32)]),
        compiler_params=_ARB(2), name="decode_attention_keylane",
    )(page_table, *args)
    return out.reshape(b, n_kv, rep, dh)


def _decode_keylane_mxu_kernel(pt_ref, q_ref, bias_ref, ks_ref, vs_ref, *rest, n_blk):
    k_refs, v_refs = rest[:n_blk], rest[n_blk:2 * n_blk]
    o_ref, m_sc, l_sc, acc_sc = rest[2 * n_blk:]
    j = pl.program_id(1)

    @pl.when(j == 0)
    def _():
        m_sc[...] = jnp.full(m_sc.shape, NEG, F32)
        l_sc[...] = jnp.zeros(l_sc.shape, F32)
        acc_sc[...] = jnp.zeros(acc_sc.shape, F32)

    q = q_ref[0]
    q16 = q.astype(BF16)
    m, l, acc = m_sc[...], l_sc[...], acc_sc[...]
    group = 8
    for i0 in range(0, n_blk, group):
        idx = range(i0, min(i0 + group, n_blk))
        ss = [lax.dot_general(q16, k_refs[i][...].astype(BF16), (((2,), (1,)), ((0,), (0,))),
                              preferred_element_type=F32) + bias_ref[0, i] for i in idx]
        m_new = m
        for s in ss:
            m_new = jnp.maximum(m_new, jnp.max(s, axis=-1, keepdims=True))
        alpha = jnp.exp(m - m_new)
        l = alpha * l
        acc = alpha * acc
        for i, s in zip(idx, ss):
            p = jnp.exp(s - m_new)
            l = l + jnp.sum(p, axis=-1, keepdims=True)
            acc = acc + lax.dot_general(p.astype(BF16), v_refs[i][...].astype(BF16), (((2,), (2,)), ((0,), (0,))),
                                        preferred_element_type=F32)
        m = m_new
    m_sc[...], l_sc[...], acc_sc[...] = m, l, acc

    @pl.when(j == pl.num_programs(1) - 1)
    def _():
        m, l, acc = m_sc[...], l_sc[...], acc_sc[...]
        s = jnp.sum(q * ks_ref[0], axis=-1, keepdims=True)
        m_new = jnp.maximum(m, s)
        alpha = jnp.exp(m - m_new)
        p = jnp.exp(s - m_new)
        o_ref[0] = (alpha * acc + p * vs_ref[0]) / jnp.maximum(alpha * l + p, 1e-30)


def decode_attention_keylane_mxu(q, k_pool, v_pool, bias, k_self, v_self, page_table):
    b, n_kv, rep, dh = q.shape
    n_pages = page_table.shape[1]
    psz = k_pool.shape[-1]
    rows = -(-rep // 8) * 8
    n_blk = min(4 * PAGES_PER_STEP, n_pages)
    qp = jnp.pad(q, ((0, 0), (0, 0), (0, rows - rep), (0, 0)))
    bias5 = bias.reshape(b, n_kv, n_pages, 1, psz).transpose(0, 2, 1, 3, 4)
    args = [qp, bias5, k_self.reshape(b, n_kv, 1, dh), v_self.reshape(b, n_kv, 1, dh)]
    in_specs = [pl.BlockSpec((1, n_kv, rows, dh), lambda bi, j, pt: (bi, 0, 0, 0)),
                pl.BlockSpec((1, n_blk, n_kv, 1, psz), lambda bi, j, pt: (bi, j, 0, 0, 0)),
                pl.BlockSpec((1, n_kv, 1, dh), lambda bi, j, pt: (bi, 0, 0, 0)),
                pl.BlockSpec((1, n_kv, 1, dh), lambda bi, j, pt: (bi, 0, 0, 0))]
    for pool in (k_pool, v_pool):
        for i in range(n_blk):
            args.append(pool)
            in_specs.append(pl.BlockSpec((None, n_kv, dh, psz),
                                         lambda bi, j, pt, i=i: (pt[bi, j * n_blk + i], 0, 0, 0)))
    out = pl.pallas_call(
        functools.partial(_decode_keylane_mxu_kernel, n_blk=n_blk),
        out_shape=jax.ShapeDtypeStruct((b, n_kv, rows, dh), F32),
        grid_spec=pltpu.PrefetchScalarGridSpec(
            num_scalar_prefetch=1, grid=(b, n_pages // n_blk), in_specs=in_specs,
            out_specs=pl.BlockSpec((1, n_kv, rows, dh), lambda bi, j, pt: (bi, 0, 0, 0)),
            scratch_shapes=[pltpu.VMEM((n_kv, rows, 1), F32), pltpu.VMEM((n_kv, rows, 1), F32),
                            pltpu.VMEM((n_kv, rows, dh), F32)]),
        compiler_params=_ARB(2), name="decode_attention_keylane_mxu",
    )(page_table, *args)
    return out[:, :, :rep]


def _decode_diff_kernel(pt_ref, q_ref, ks_ref, vs_ref, e_ref, *rest, n_blk, n_heads):
    k_refs, v_refs = rest[:n_blk], rest[n_blk:2 * n_blk]
    o_ref, m_sc, l_sc, acc_sc = rest[2 * n_blk:]
    j = pl.program_id(1)

    @pl.when(j == 0)
    def _():
        m_sc[...] = jnp.full(m_sc.shape, NEG, F32)
        l_sc[...] = jnp.zeros(l_sc.shape, F32)
        acc_sc[...] = jnp.zeros(acc_sc.shape, F32)

    q = q_ref[0]
    qb = jnp.broadcast_to(q, q.shape[:2] + (LANES,))
    n_sub = q.shape[0]
    wide = e_ref.shape[1]
    own = (lax.broadcasted_iota(jnp.int32, (n_sub, wide), 1) % n_heads
           == lax.broadcasted_iota(jnp.int32, (n_sub, wide), 0) // (n_sub // n_heads))
    m, l, acc = m_sc[...], l_sc[...], acc_sc[...]
    for i in range(n_blk):
        s = jnp.sum(qb * k_refs[i][...], axis=1)
        m_new = jnp.maximum(m, jnp.max(s, axis=-1, keepdims=True))
        alpha = jnp.exp(m - m_new)
        p = jnp.exp(s - m_new)
        l = alpha * l + jnp.sum(p, axis=-1, keepdims=True)
        spread = jnp.dot(p.astype(BF16), e_ref[...], preferred_element_type=F32)
        spread = jnp.where(own, spread, 0.0).astype(BF16)
        acc = alpha * acc + jnp.dot(spread, v_refs[i][...].astype(BF16), preferred_element_type=F32)
        m = m_new
    m_sc[...], l_sc[...], acc_sc[...] = m, l, acc

    @pl.when(j == pl.num_programs(1) - 1)
    def _():
        m, l, acc = m_sc[...], l_sc[...], acc_sc[...]
        s = jnp.sum(q * ks_ref[0], axis=1)
        m_new = jnp.maximum(m, s)
        alpha = jnp.exp(m - m_new)
        p = jnp.exp(s - m_new)
        o_ref[0] = (alpha * acc + p * vs_ref[0]) / jnp.maximum(alpha * l + p, 1e-30)


def decode_attention_diff(q, k_pool, v_pool, k_self, v_self, page_table):
    b, n_sub, dh = q.shape
    n_pages = page_table.shape[1]
    psz, n_heads, dv = v_pool.shape[1:]
    n_blk = PAGES_PER_STEP
    v_rows = v_pool.reshape(v_pool.shape[0], psz * n_heads, dv)
    expand = jnp.asarray(np.arange(psz)[:, None] == np.arange(psz * n_heads)[None, :] // n_heads, dtype=BF16)
    args = [q.reshape(b, n_sub, dh, 1), k_self.reshape(b, n_sub, dh, 1), v_self, expand]
    in_specs = [pl.BlockSpec((1, n_sub, dh, 1), lambda bi, j, pt: (bi, 0, 0, 0)),
                pl.BlockSpec((1, n_sub, dh, 1), lambda bi, j, pt: (bi, 0, 0, 0)),
                pl.BlockSpec((1, n_sub, dv), lambda bi, j, pt: (bi, 0, 0)),
                pl.BlockSpec(expand.shape, lambda bi, j, pt: (0, 0))]
    for i in range(n_blk):
        args.append(k_pool)
        in_specs.append(pl.BlockSpec((None, n_sub, dh, psz), lambda bi, j, pt, i=i: (pt[bi, j * n_blk + i], 0, 0, 0)))
    for i in range(n_blk):
        args.append(v_rows)
        in_specs.append(pl.BlockSpec((None, psz * n_heads, dv), lambda bi, j, pt, i=i: (pt[bi, j * n_blk + i], 0, 0)))
    return pl.pallas_call(
        functools.partial(_decode_diff_kernel, n_blk=n_blk, n_heads=n_heads),
        out_shape=jax.ShapeDtypeStruct((b, n_sub, dv), F32),
        grid_spec=pltpu.PrefetchScalarGridSpec(
            num_scalar_prefetch=1, grid=(b, n_pages // n_blk), in_specs=in_specs,
            out_specs=pl.BlockSpec((1, n_sub, dv), lambda bi, j, pt: (bi, 0, 0)),
            scratch_shapes=[pltpu.VMEM((n_sub, 1), F32), pltpu.VMEM((n_sub, 1), F32),
                            pltpu.VMEM((n_sub, dv), F32)]),
        compiler_params=_ARB(2), name="decode_attention_diff",
    )(page_table, *args)


def _paged_cumsum_kernel(pt_ref, tri_ref, *rest, n_blk):
    x_refs = rest[:n_blk]
    c_ref, tot_ref, carry_sc = rest[n_blk:]
    j = pl.program_id(1)

    @pl.when(j == 0)
    def _():
        carry_sc[...] = jnp.zeros(carry_sc.shape, F32)

    carry = carry_sc[...]
    psz = tri_ref.shape[0]
    nh = carry.shape[0]
    x = jnp.concatenate([r[...] for r in x_refs], axis=0)
    pre_all = jnp.dot(x, tri_ref[...], preferred_element_type=F32, precision=lax.Precision.HIGHEST)
    for i in range(n_blk):
        pre = pre_all[i * nh:(i + 1) * nh]
        c_ref[0, :, i * psz:(i + 1) * psz] = pre + carry
        carry = carry + pre[:, psz - 1:psz]
    carry_sc[...] = carry
    tot_ref[0] = jnp.broadcast_to(carry, tot_ref.shape[1:])


def paged_cumsum(pool_t, page_table):
    b, n_pages = page_table.shape
    _, nh, psz = pool_t.shape
    n_blk = min(2 * PAGES_PER_STEP, n_pages)
    tri = jnp.asarray(np.arange(psz)[:, None] <= np.arange(psz)[None, :], dtype=F32)
    in_specs = [pl.BlockSpec(tri.shape, lambda bi, j, pt: (0, 0))]
    in_specs += [pl.BlockSpec((None, nh, psz), lambda bi, j, pt, i=i: (pt[bi, j * n_blk + i], 0, 0))
                 for i in range(n_blk)]
    c, tot = pl.pallas_call(
        functools.partial(_paged_cumsum_kernel, n_blk=n_blk),
        out_shape=[jax.ShapeDtypeStruct((b, nh, n_pages * psz), F32), jax.ShapeDtypeStruct((b, nh, LANES), F32)],
        grid_spec=pltpu.PrefetchScalarGridSpec(
            num_scalar_prefetch=1, grid=(b, n_pages // n_blk), in_specs=in_specs,
            out_specs=[pl.BlockSpec((1, nh, n_blk * psz), lambda bi, j, pt: (bi, 0, j)),
                       pl.BlockSpec((1, nh, LANES), lambda bi, j, pt: (bi, 0, 0))],
            scratch_shapes=[pltpu.VMEM((nh, 1), F32)]),
        compiler_params=_ARB(2), name="paged_cumsum",
    )(page_table, tri, *([pool_t] * n_blk))
    return c, tot[:, :, 0]


def _keys_on_lanes(pool):
    return pool.transpose(0, 2, 3, 1)


def _spread_heads(q, width):
    b = q.shape[0]
    n_kv = width // HEAD_DIM
    owner = (np.arange(DEC_HEADS) * n_kv) // DEC_HEADS
    onehot = jnp.asarray(owner[:, None] == np.arange(n_kv)[None, :], dtype=F32)
    out = q[:, :, None, :] * onehot[None, :, :, None]
    return (out * SCALE).reshape(b, DEC_HEADS, width).astype(BF16)


def _own_lanes(o, width_per_head):
    b, nh, lv = o.shape
    n_kv = lv // width_per_head
    owner = (np.arange(nh) * n_kv) // nh
    o = o.reshape(b, nh, n_kv, width_per_head)
    return jnp.take_along_axis(o, jnp.asarray(owner).reshape(1, nh, 1, 1), axis=2)[:, :, 0]


def _diff_finish_kernel(o0_ref, o1_ref, lam_ref, subln_ref, o_ref, *, lam_init):
    o = o0_ref[...] - lam_ref[...] * o1_ref[...]
    o_ref[...] = _rms(o, subln_ref[...]) * (1.0 - lam_init)


def diff_finish(o0, o1, lam, lam_init, subln):
    m, w = o0.shape
    lam_vec = jnp.broadcast_to(lam.astype(F32).reshape(1, 1), (1, w))
    full = lambda shape: pl.BlockSpec(shape, lambda i: (0, 0))
    return pl.pallas_call(
        functools.partial(_diff_finish_kernel, lam_init=lam_init),
        grid=(1,), in_specs=[full((m, w)), full((m, w)), full((1, w)), full((1, w))],
        out_specs=full((m, w)), out_shape=jax.ShapeDtypeStruct((m, w), F32),
        compiler_params=_ARB(1), name="diff_finish",
    )(o0, o1, lam_vec, subln.reshape(1, w))


def _nsa_cmp_decode_kernel(*refs, n_blk, ncp, ns, past):
    it = iter(refs)
    next(it)
    qt_ref = next(it)
    pages = [[next(it) for _ in range(n_blk)] for _ in range(2)]
    pek_ref, w1k_ref, w2k_ref, pev_ref, w1v_ref, w2v_ref, ov_ref = (next(it) for _ in range(7))
    o_ref, sel_ref = next(it), next(it)
    ph_refs = [next(it), next(it)]
    stage_ref = next(it)
    j = pl.program_id(1)
    for t in range(2):
        for half in range(2):
            for i in range(n_blk):
                pair = jnp.concatenate([pages[t][i][2 * half], pages[t][i][2 * half + 1]], axis=0)
                stage_ref[...] = pair.T
                c0 = pl.multiple_of((j * n_blk + i) * 8, 8)
                for l in range(NSA_CMP_STRIDE):
                    ph_refs[t][l, pl.ds(c0, 8), half * LANES:(half + 1) * LANES] = \
                        stage_ref[pl.ds(l, 8, stride=NSA_CMP_STRIDE), :]

    @pl.when(j == pl.num_programs(1) - 1)
    def _():
        kc = _compress_core(lambda l: ph_refs[0][l], pek_ref, w1k_ref, w2k_ref, ncp)
        vc = _compress_core(lambda l: ph_refs[1][l], pev_ref, w1v_ref, w2v_ref, ncp)
        qt = qt_ref[0]
        s = lax.dot_general(qt, kc.astype(BF16), (((1,), (1,)), ((), ())), preferred_element_type=F32)
        valid = lax.broadcasted_iota(jnp.int32, s.shape, 1) < ncp - 1
        s = jnp.where(valid, s, NEG)
        e = jnp.where(valid, jnp.exp(s - jnp.max(s, axis=-1, keepdims=True)), 0.0)
        p = e / jnp.maximum(jnp.sum(e, axis=-1, keepdims=True), 1e-30)
        o_ref[0] = jnp.dot(p.astype(BF16), vc.astype(BF16), preferred_element_type=F32)
        grp = (lax.broadcasted_iota(jnp.int32, (8, DEC_HEADS), 1) // NSA_REP
               == lax.broadcasted_iota(jnp.int32, (8, DEC_HEADS), 0)).astype(F32)
        psum = jnp.dot(grp, p, preferred_element_type=F32, precision=lax.Precision.HIGHEST)
        imp = jnp.dot(psum, ov_ref[...], preferred_element_type=F32, precision=lax.Precision.HIGHEST)
        sel = _select_blocks(imp, jnp.full((8, 1), past, jnp.int32), ns)
        sel_ref[0] = jnp.where(sel, 1.0, 0.0)


def nsa_cmp_decode(qt, pool_k, pool_v, page_table, cw_k, cw_v):
    b = qt.shape[0]
    n_pages = page_table.shape[1]
    past = n_pages * pool_k.shape[-1]
    n_blk = PAGES_PER_STEP
    ncp = past // NSA_CMP_STRIDE
    ns = past // NSA_SEL_LEN + 1
    nsp = -(-ns // LANES) * LANES
    ov = jnp.asarray(np.pad(nsa_overlap(ncp, ns), ((0, 0), (0, nsp - ns))))
    args = [qt]
    in_specs = [pl.BlockSpec((1, DEC_HEADS, 4 * HEAD_DIM), lambda bi, j, pt: (bi, 0, 0))]
    for pool in (pool_k, pool_v):
        for i in range(n_blk):
            args.append(pool)
            in_specs.append(pl.BlockSpec((None,) + pool.shape[1:],
                                         lambda bi, j, pt, i=i: (pt[bi, j * n_blk + i], 0, 0, 0)))
    for a in tuple(cw_k) + tuple(cw_v) + (ov,):
        args.append(a)
        in_specs.append(pl.BlockSpec(a.shape, lambda bi, j, pt, nd=a.ndim: (0,) * nd))
    return pl.pallas_call(
        functools.partial(_nsa_cmp_decode_kernel, n_blk=n_blk, ncp=ncp, ns=ns, past=past),
        out_shape=[jax.ShapeDtypeStruct((b, DEC_HEADS, 4 * HEAD_DIM), F32),
                   jax.ShapeDtypeStruct((b, 8, nsp), F32)],
        grid_spec=pltpu.PrefetchScalarGridSpec(
            num_scalar_prefetch=1, grid=(b, n_pages // n_blk), in_specs=in_specs,
            out_specs=[pl.BlockSpec((1, DEC_HEADS, 4 * HEAD_DIM), lambda bi, j, pt: (bi, 0, 0)),
                       pl.BlockSpec((1, 8, nsp), lambda bi, j, pt: (bi, 0, 0))],
            scratch_shapes=[pltpu.VMEM((NSA_CMP_STRIDE, ncp, 4 * HEAD_DIM), F32)] * 2
                           + [pltpu.VMEM((pool_k.shape[-1], LANES), F32)]),
        compiler_params=_ARB(2), name="nsa_cmp_decode",
    )(page_table, *args)


def _pad_cols(w, n):
    return jnp.pad(w, ((0, 0), (0, n - w.shape[1])))


def _pad_vec(v, n):
    return jnp.pad(v, (0, n - v.shape[0]))


def fox_mixer(hp, hs, g, cache_k, cache_v, cache_logf, page_table, w_in, b_f, w_o):
    bp, sp, d = hp.shape
    bs, ds, _ = hs.shape
    past = page_table.shape[1] * cache_k.shape[1]
    hd = FOX_HEADS * HEAD_DIM
    wb = w_in.astype(BF16)
    ws = [wb[:, :hd], wb[:, hd:2 * hd], wb[:, 2 * hd:3 * hd], _pad_cols(wb[:, 3 * hd:], LANES)]
    biases = [None, None, None, _pad_vec(b_f, LANES)]
    wo = w_o.astype(BF16)
    hp2 = hp.reshape(bp * sp, d)
    hs2 = hs.reshape(bs * ds, d)
    cfgs = [ProjCfg(f32=False, bf16=True, scale=SCALE), ProjCfg(bf16=True), ProjCfg(bf16=True),
            ProjCfg(act="logsig")]
    q16, k, k16, v, v16, lf = norm_proj(hp2, g, ws, cfgs, biases)
    logf = lf[:, :FOX_HEADS].reshape(bp, sp, FOX_HEADS)
    c = jnp.cumsum(logf, axis=1)
    o_p = fox_prompt_attention(q16.reshape(bp, sp, hd), k16.reshape(bp, sp, hd), v16.reshape(bp, sp, hd), c)
    hp_new = out_proj([o_p.reshape(bp * sp, hd)], wo, hp2).reshape(bp, sp, d)
    sh = (bp, sp, FOX_HEADS, HEAD_DIM)
    k, v = k.reshape(sh), v.reshape(sh)
    cfgs = [ProjCfg(), ProjCfg(), ProjCfg(), ProjCfg(act="logsig")]
    qs, ks, vs, lfs = norm_proj(hs2, g, ws, cfgs, biases)
    logfs = lfs[:, :FOX_HEADS].reshape(bs, ds, FOX_HEADS)
    c_past, c_tot = paged_cumsum(cache_logf.transpose(0, 2, 1), page_table)
    bias = (c_tot + logfs[:, 0])[:, :, None] - c_past
    hsh = (bs, FOX_HEADS, HEAD_DIM)
    o_s = decode_attention_keylane((qs * SCALE).reshape(bs, FOX_HEADS, 1, HEAD_DIM), _keys_on_lanes(cache_k),
                                   _keys_on_lanes(cache_v), bias, ks.reshape(hsh), vs.reshape(hsh), page_table)
    hs_new = out_proj([o_s.reshape(bs * ds, D_MODEL)], wo, hs2).reshape(bs, ds, d)
    shs = (bs, ds, FOX_HEADS, HEAD_DIM)
    ks, vs = ks.reshape(shs), vs.reshape(shs)
    return hp_new, hs_new, (k, v, logf, ks, vs, logfs)


def mlp_both(hp, hs, g, wu, wd):
    wu = wu.astype(BF16)
    wd = wd.astype(BF16)
    yp = mlp(hp.reshape(-1, D_MODEL), g, wu, wd).reshape(hp.shape)
    ys = mlp(hs.reshape(-1, D_MODEL), g, wu, wd).reshape(hs.shape)
    return yp, ys


def nsa_mixer(hp, hs, gn, cache_cmp_k, cache_cmp_v, cache_sel_k, cache_sel_v, state_win_k, state_win_v,
              page_table, w_in, b_gate, pe_k, w1_k, w2_k, pe_v, w1_v, w2_v, w_o):
    bp, sp, d = hp.shape
    bs, ds, _ = hs.shape
    past = page_table.shape[1] * cache_cmp_k.shape[1]
    wb_ = state_win_k.shape[1]
    hq = NSA_HEADS * HEAD_DIM
    kvd = NSA_GROUPS * HEAD_DIM
    wbf = w_in.astype(BF16)
    ws = [wbf[:, :hq]] + [wbf[:, hq + i * kvd:hq + (i + 1) * kvd] for i in range(6)] \
        + [_pad_cols(wbf[:, hq + 6 * kvd:], LANES)]
    biases = [None] * 7 + [_pad_vec(b_gate, LANES)]
    wo = w_o.astype(BF16)
    hp2 = hp.reshape(bp * sp, d)
    hs2 = hs.reshape(bs * ds, d)
    cfgs = [ProjCfg(rope=True, f32=False, bf16=True, scale=SCALE),
            ProjCfg(rope=True), ProjCfg(),
            ProjCfg(rope=True, bf16=True), ProjCfg(bf16=True),
            ProjCfg(rope=True, bf16=True), ProjCfg(bf16=True),
            ProjCfg(act="sigmoid")]
    tabs = rope_tables(jnp.arange(sp, dtype=jnp.int32))
    (q16, kc, vc, ksl, ksl16, vsl, vsl16, kw, kw16, vw, vw16, gt) = norm_proj(hp2, gn, ws, cfgs, biases, tabs)
    r3 = lambda a: a.reshape(bp, sp, a.shape[-1])
    q16 = r3(q16)
    gates = gt[:, :3 * NSA_HEADS].reshape(bp, sp, 3, NSA_HEADS)
    g_cmp = gates[:, :, 0].reshape(bp, sp, NSA_GROUPS, NSA_REP).transpose(0, 2, 1, 3)
    g_sel = gates[:, :, 1].reshape(bp, sp, 2, 8).transpose(0, 2, 3, 1)
    g_win = gates[:, :, 2].reshape(bp, sp, 2, 8).transpose(0, 2, 3, 1)
    kcc = nsa_compress_prompt(r3(kc), nsa_compress_weights(pe_k, w1_k, w2_k))
    vcc = nsa_compress_prompt(r3(vc), nsa_compress_weights(pe_v, w1_v, w2_v))
    o_cmp, sel = nsa_cmp_prompt(q16, kcc, vcc, g_cmp)
    o_sel = nsa_branch_prompt(q16, r3(ksl16), r3(vsl16), g_sel, sel)
    o_win = nsa_branch_prompt(q16, r3(kw16), r3(vw16), g_win)
    hp_new = out_proj([o_cmp.reshape(bp * sp, d), o_sel.reshape(bp * sp, d), o_win.reshape(bp * sp, d)],
                      wo, hp2).reshape(bp, sp, d)
    r4 = lambda a: a.reshape(bp, sp, NSA_GROUPS, HEAD_DIM)
    kc, vc, ksl, vsl, kw, vw = r4(kc), r4(vc), r4(ksl), r4(vsl), r4(kw), r4(vw)
    cfgs = [ProjCfg(rope=True), ProjCfg(rope=True), ProjCfg(), ProjCfg(rope=True), ProjCfg(),
            ProjCfg(rope=True), ProjCfg(), ProjCfg(act="sigmoid")]
    tabs_s = rope_tables(jnp.full((bs * ds,), past, jnp.int32))
    qs, kc_s, vc_s, ksl_s, vsl_s, kw_s, vw_s, gt_s = norm_proj(hs2, gn, ws, cfgs, biases, tabs_s)
    r4s = lambda a: a.reshape(bs, ds, NSA_GROUPS, HEAD_DIM)
    kc_s, vc_s, ksl_s, vsl_s, kw_s, vw_s = r4s(kc_s), r4s(vc_s), r4s(ksl_s), r4s(vsl_s), r4s(kw_s), r4s(vw_s)
    gates_s = gt_s[:, :3 * NSA_HEADS].reshape(bs, ds, 3, NSA_HEADS)
    kvd4 = NSA_GROUPS * HEAD_DIM
    row = lambda a: a.reshape(bs, 1, kvd4)
    qt = _spread_heads(qs.reshape(bs, NSA_HEADS, HEAD_DIM), kvd4)
    o_cmp_full, selm = nsa_cmp_decode(qt, _keys_on_lanes(cache_cmp_k), _keys_on_lanes(cache_cmp_v), page_table,
                                      nsa_compress_weights(pe_k, w1_k, w2_k), nsa_compress_weights(pe_v, w1_v, w2_v))
    n_past_blk = past // NSA_SEL_LEN
    sel_keys = jnp.repeat(selm[:, :NSA_GROUPS, :n_past_blk], NSA_SEL_LEN, axis=2)
    bias_sel = jnp.where(sel_keys > 0.5, 0.0, NEG)
    gsh = (bs, NSA_GROUPS, HEAD_DIM)
    o_sel = decode_attention_keylane_mxu((qs * SCALE).reshape(bs, NSA_GROUPS, NSA_REP, HEAD_DIM), _keys_on_lanes(cache_sel_k),
                                         _keys_on_lanes(cache_sel_v), bias_sel, ksl_s.reshape(gsh), vsl_s.reshape(gsh),
                                         page_table).reshape(bs, NSA_HEADS, HEAD_DIM)
    in_win = jnp.arange(wb_) > wb_ - NSA_WINDOW
    bias_win = jnp.broadcast_to(jnp.where(in_win, 0.0, NEG)[None, None, :], (bs, NSA_HEADS, wb_)).astype(F32)
    o_win_full = decode_attention(qt, state_win_k.reshape(bs, wb_, kvd4), state_win_v.reshape(bs, wb_, kvd4),
                                  bias_win, row(kw_s), row(vw_s))
    gs = gates_s.reshape(bs, 3, NSA_HEADS, 1)
    o_s = (gs[:, 0] * _own_lanes(o_cmp_full, HEAD_DIM) + gs[:, 1] * o_sel
           + gs[:, 2] * _own_lanes(o_win_full, HEAD_DIM))
    hs_new = out_proj([o_s.reshape(bs * ds, d)], wo, hs2).reshape(bs, ds, d)
    kw_all = jnp.concatenate([state_win_k, kw_s], axis=1)
    vw_all = jnp.concatenate([state_win_v, vw_s], axis=1)
    keep = min(NSA_WINDOW, sp)
    return hp_new, hs_new, (kc, vc, ksl, vsl, kw[:, sp - keep:], vw[:, sp - keep:],
                            kc_s, vc_s, ksl_s, vsl_s, kw_all[:, ds:], vw_all[:, ds:])


def diff_mixer(hp, hs, gn, cache_k, cache_v, page_table, w_in, lq1, lk1, lq2, lk2, subln, w_o, layer_idx):
    lam_init = 0.8 - 0.6 * math.exp(-0.3 * layer_idx)
    lam = (jnp.exp(jnp.sum(lq1.astype(F32) * lk1.astype(F32)))
           - jnp.exp(jnp.sum(lq2.astype(F32) * lk2.astype(F32))) + lam_init)
    bp, sp, d = hp.shape
    bs, ds, _ = hs.shape
    past = page_table.shape[1] * cache_k.shape[1]
    wbf = w_in.astype(BF16)
    ws = [wbf[:, :d], wbf[:, d:2 * d], wbf[:, 2 * d:]]
    wo = w_o.astype(BF16)
    hp2 = hp.reshape(bp * sp, d)
    hs2 = hs.reshape(bs * ds, d)
    cfgs = [ProjCfg(rope=True, f32=False, bf16=True, scale=SCALE), ProjCfg(rope=True, bf16=True),
            ProjCfg(bf16=True)]
    tabs = rope_tables(jnp.arange(sp, dtype=jnp.int32))
    q16, k, k16, v, v16 = norm_proj(hp2, gn, ws, cfgs, None, tabs)
    r3 = lambda a: a.reshape(bp, sp, d)
    o_p = diff_prompt_attention(r3(q16), r3(k16), r3(v16), lam, lam_init, subln)
    hp_new = out_proj([o_p.reshape(bp * sp, d)], wo, hp2).reshape(bp, sp, d)
    k = k.reshape(bp, sp, 2 * DIFF_HEADS, HEAD_DIM)
    v = v.reshape(bp, sp, DIFF_HEADS, 2 * HEAD_DIM)
    cfgs = [ProjCfg(rope=True), ProjCfg(rope=True), ProjCfg()]
    tabs_s = rope_tables(jnp.full((bs * ds,), past, jnp.int32))
    qs, ks, vs = norm_proj(hs2, gn, ws, cfgs, None, tabs_s)
    o_sub = decode_attention_diff((qs * SCALE).reshape(bs, 2 * DIFF_HEADS, HEAD_DIM), _keys_on_lanes(cache_k), cache_v,
                                  ks.reshape(bs, 2 * DIFF_HEADS, HEAD_DIM),
                                  jnp.repeat(vs.reshape(bs, DIFF_HEADS, 2 * HEAD_DIM), 2, axis=1), page_table)
    o_s = diff_finish(o_sub[:, 0::2].reshape(bs * DIFF_HEADS, 2 * HEAD_DIM),
                      o_sub[:, 1::2].reshape(bs * DIFF_HEADS, 2 * HEAD_DIM), lam, lam_init, subln)
    hs_new = out_proj([o_s.reshape(bs * ds, D_MODEL)], wo, hs2).reshape(bs, ds, d)
    ks = ks.reshape(bs, ds, 2 * DIFF_HEADS, HEAD_DIM)
    vs = vs.reshape(bs, ds, DIFF_HEADS, 2 * HEAD_DIM)
    return hp_new, hs_new, (k, v, ks, vs)


def kernel(x_prompt, x_sample, cache_l0_k, cache_l0_v, cache_l0_logf, cache_l1_cmp_k, cache_l1_cmp_v, cache_l1_sel_k, cache_l1_sel_v, state_l1_win_k, state_l1_win_v, cache_l2_k, cache_l2_v, cache_l3_k, cache_l3_v, cache_l3_logf, page_table, l0_norm_mix, l0_fox_w_in, l0_fox_b_f, l0_fox_w_o, l0_norm_mlp, l0_mlp_up, l0_mlp_down, l1_norm_mix, l1_nsa_w_in, l1_nsa_b_gate, l1_nsa_pe_k, l1_nsa_w1_k, l1_nsa_w2_k, l1_nsa_pe_v, l1_nsa_w1_v, l1_nsa_w2_v, l1_nsa_w_o, l1_norm_mlp, l1_mlp_up, l1_mlp_down, l2_norm_mix, l2_diff_w_in, l2_diff_lq1, l2_diff_lk1, l2_diff_lq2, l2_diff_lk2, l2_diff_subln, l2_diff_w_o, l2_norm_mlp, l2_mlp_up, l2_mlp_down, l3_norm_mix, l3_fox_w_in, l3_fox_b_f, l3_fox_w_o, l3_norm_mlp, l3_mlp_up, l3_mlp_down, norm_final):
    hp, hs = x_prompt, x_sample
    state = []
    hp, hs, st = fox_mixer(hp, hs, l0_norm_mix, cache_l0_k, cache_l0_v, cache_l0_logf, page_table,
                           l0_fox_w_in, l0_fox_b_f, l0_fox_w_o)
    state += st
    hp, hs = mlp_both(hp, hs, l0_norm_mlp, l0_mlp_up, l0_mlp_down)
    hp, hs, st = nsa_mixer(hp, hs, l1_norm_mix, cache_l1_cmp_k, cache_l1_cmp_v, cache_l1_sel_k, cache_l1_sel_v,
                           state_l1_win_k, state_l1_win_v, page_table,
                           l1_nsa_w_in, l1_nsa_b_gate, l1_nsa_pe_k, l1_nsa_w1_k, l1_nsa_w2_k,
                           l1_nsa_pe_v, l1_nsa_w1_v, l1_nsa_w2_v, l1_nsa_w_o)
    state += st
    hp, hs = mlp_both(hp, hs, l1_norm_mlp, l1_mlp_up, l1_mlp_down)
    hp, hs, st = diff_mixer(hp, hs, l2_norm_mix, cache_l2_k, cache_l2_v, page_table, l2_diff_w_in,
                            l2_diff_lq1, l2_diff_lk1, l2_diff_lq2, l2_diff_lk2,
                            l2_diff_subln, l2_diff_w_o, DIFF_LAYER)
    state += st
    hp, hs = mlp_both(hp, hs, l2_norm_mlp, l2_mlp_up, l2_mlp_down)
    hp, hs, st = fox_mixer(hp, hs, l3_norm_mix, cache_l3_k, cache_l3_v, cache_l3_logf, page_table,
                           l3_fox_w_in, l3_fox_b_f, l3_fox_w_o)
    state += st
    hp, hs = mlp_both(hp, hs, l3_norm_mlp, l3_mlp_up, l3_mlp_down)
    y_p = final_norm(hp.reshape(-1, D_MODEL), norm_final).reshape(hp.shape)
    y_s = final_norm(hs.reshape(-1, D_MODEL), norm_final).reshape(hs.shape)
    return (y_p, y_s) + tuple(state)
```

```python
import functools
import math
from typing import NamedTuple

import numpy as np
import jax
import jax.numpy as jnp
from jax import lax
from jax.experimental import pallas as pl
from jax.experimental.pallas import tpu as pltpu

F32 = jnp.float32
BF16 = jnp.bfloat16

D_MODEL = 1024
HEAD_DIM = 64
FOX_HEADS = 16
NSA_HEADS = 16
NSA_GROUPS = 4
NSA_REP = 4
NSA_CMP_LEN = 32
NSA_CMP_STRIDE = 16
NSA_SEL_LEN = 64
NSA_TOP_N = 16
NSA_N_LOCAL = 2
NSA_WINDOW = 512
DIFF_HEADS = 8
D_FF = 4096
ROPE_THETA = 10000.0
NORM_EPS = 1e-6
DIFF_LAYER = 2
SCALE = HEAD_DIM ** -0.5
LOG2E = math.log2(math.e)
Q_SCALE_LOG2 = SCALE * LOG2E

LANES = 128
VMEM_LIMIT = 56 * 1024 * 1024
ROW_TILE = 512
NEG = -1e30

_ARB = lambda n: pltpu.CompilerParams(dimension_semantics=("arbitrary",) * n,
                                      vmem_limit_bytes=VMEM_LIMIT)


def _row_tile(m):
    return ROW_TILE if m % ROW_TILE == 0 else m


def _rms(x, g):
    return x * lax.rsqrt(jnp.mean(x * x, axis=-1, keepdims=True) + NORM_EPS) * g


def _swap_halves(x):
    parts = []
    for c in range(x.shape[-1] // LANES):
        parts += [x[:, c * LANES + 64:(c + 1) * LANES], x[:, c * LANES:c * LANES + 64]]
    return jnp.concatenate(parts, axis=-1)


class ProjCfg(NamedTuple):
    rope: bool = False
    f32: bool = True
    bf16: bool = False
    scale: float = 1.0
    act: str = ""


def _rope(y, cos, sin):
    n = y.shape[-1]
    reps = n // LANES
    if reps > 1:
        cos = jnp.concatenate([cos] * reps, axis=-1)
        sin = jnp.concatenate([sin] * reps, axis=-1)
    lane = lax.broadcasted_iota(jnp.int32, y.shape, 1)
    partner = jnp.where((lane % HEAD_DIM) < HEAD_DIM // 2,
                        pltpu.roll(y, n - HEAD_DIM // 2, 1), pltpu.roll(y, HEAD_DIM // 2, 1))
    return y * cos + partner * sin


def _norm_proj_kernel(*refs, cfgs, use_rope):
    it = iter(refs)
    x_ref, g_ref = next(it), next(it)
    if use_rope:
        cos_ref, sin_ref = next(it), next(it)
    ins = []
    for c in cfgs:
        w_ref = next(it)
        ins.append((w_ref, next(it) if c.act else None))
    xn = _rms(x_ref[...], g_ref[...]).astype(BF16)
    for c, (w_ref, b_ref) in zip(cfgs, ins):
        y = jnp.dot(xn, w_ref[...], preferred_element_type=F32)
        if c.rope:
            y = _rope(y, cos_ref[...], sin_ref[...])
        if c.act:
            z = y + b_ref[...]
            if c.act == "logsig":
                y = jnp.minimum(z, 0.0) - jnp.log1p(jnp.exp(-jnp.abs(z)))
            else:
                y = 1.0 / (1.0 + jnp.exp(-z))
        if c.f32:
            next(it)[...] = y
        if c.bf16:
            next(it)[...] = (y * c.scale).astype(BF16)


def norm_proj(x, g, ws, cfgs, biases=None, rope_tables=None):
    m, d = x.shape
    tm = _row_tile(m)
    use_rope = rope_tables is not None
    args = [x, g.reshape(1, d)]
    in_specs = [pl.BlockSpec((tm, d), lambda i: (i, 0)), pl.BlockSpec((1, d), lambda i: (0, 0))]
    if use_rope:
        nt = rope_tables[0].shape[0] // tm
        for t in rope_tables:
            args.append(t)
            in_specs.append(pl.BlockSpec((tm, LANES), lambda i: (i % nt, 0)))
    out_specs, out_shape = [], []
    for ci, (w, c) in enumerate(zip(ws, cfgs)):
        n = w.shape[1]
        args.append(w)
        in_specs.append(pl.BlockSpec(w.shape, lambda i: (0, 0)))
        if c.act:
            args.append(biases[ci].reshape(1, n))
            in_specs.append(pl.BlockSpec((1, n), lambda i: (0, 0)))
        for flag, dt in ((c.f32, F32), (c.bf16, BF16)):
            if flag:
                out_specs.append(pl.BlockSpec((tm, n), lambda i: (i, 0)))
                out_shape.append(jax.ShapeDtypeStruct((m, n), dt))
    return pl.pallas_call(
        functools.partial(_norm_proj_kernel, cfgs=tuple(cfgs), use_rope=use_rope),
        grid=(m // tm,), in_specs=in_specs, out_specs=out_specs, out_shape=out_shape,
        compiler_params=_ARB(1), name="norm_proj",
    )(*args)


def _out_proj_kernel(*refs, n_in):
    a_refs = refs[:n_in]
    w_ref, res_ref, o_ref = refs[n_in:]
    a = a_refs[0][...]
    for r in a_refs[1:]:
        a = a + r[...]
    o_ref[...] = res_ref[...] + jnp.dot(a.astype(BF16), w_ref[...], preferred_element_type=F32)


def out_proj(a_list, w, res):
    m, d = res.shape
    tm = _row_tile(m)
    k = w.shape[0]
    return pl.pallas_call(
        functools.partial(_out_proj_kernel, n_in=len(a_list)),
        grid=(m // tm,),
        in_specs=[pl.BlockSpec((tm, k), lambda i: (i, 0)) for _ in a_list]
                 + [pl.BlockSpec(w.shape, lambda i: (0, 0)),
                    pl.BlockSpec((tm, d), lambda i: (i, 0))],
        out_specs=pl.BlockSpec((tm, d), lambda i: (i, 0)),
        out_shape=jax.ShapeDtypeStruct((m, d), F32),
        compiler_params=_ARB(1), name="out_proj",
    )(*a_list, w, res)


FF_CHUNK = 1024


def _mlp_kernel(x_ref, g_ref, wu_ref, wd_ref, o_ref):
    x = x_ref[...]
    xn = _rms(x, g_ref[...]).astype(BF16)
    acc = x
    for c in range(D_FF // FF_CHUNK):
        h = jnp.dot(xn, wu_ref[:, c * FF_CHUNK:(c + 1) * FF_CHUNK], preferred_element_type=F32)
        h = jnp.maximum(h, 0.0)
        acc = acc + jnp.dot((h * h).astype(BF16), wd_ref[c * FF_CHUNK:(c + 1) * FF_CHUNK, :],
                            preferred_element_type=F32)
    o_ref[...] = acc


def mlp(x, g, wu, wd):
    m, d = x.shape
    tm = _row_tile(m)
    return pl.pallas_call(
        _mlp_kernel,
        grid=(m // tm,),
        in_specs=[pl.BlockSpec((tm, d), lambda i: (i, 0)),
                  pl.BlockSpec((1, d), lambda i: (0, 0)),
                  pl.BlockSpec(wu.shape, lambda i: (0, 0)),
                  pl.BlockSpec(wd.shape, lambda i: (0, 0))],
        out_specs=pl.BlockSpec((tm, d), lambda i: (i, 0)),
        out_shape=jax.ShapeDtypeStruct((m, d), F32),
        compiler_params=_ARB(1), name="mlp",
    )(x, g.reshape(1, d), wu, wd)


def _final_norm_kernel(x_ref, g_ref, o_ref):
    o_ref[...] = _rms(x_ref[...], g_ref[...])


def final_norm(x, g):
    m, d = x.shape
    tm = _row_tile(m)
    return pl.pallas_call(
        _final_norm_kernel,
        grid=(m // tm,),
        in_specs=[pl.BlockSpec((tm, d), lambda i: (i, 0)),
                  pl.BlockSpec((1, d), lambda i: (0, 0))],
        out_specs=pl.BlockSpec((tm, d), lambda i: (i, 0)),
        out_shape=jax.ShapeDtypeStruct((m, d), F32),
        compiler_params=_ARB(1), name="final_norm",
    )(x, g.reshape(1, d))


def rope_tables(pos):
    half = HEAD_DIM // 2
    inv = ROPE_THETA ** (-jnp.arange(half, dtype=F32) / half)
    ang = pos.astype(F32)[:, None] * inv[None, :]
    cos, sin = jnp.cos(ang), jnp.sin(ang)
    cos = jnp.concatenate([cos, cos, cos, cos], axis=-1)
    sin = jnp.concatenate([-sin, sin, -sin, sin], axis=-1)
    return cos, sin


def _flash_streams(qs, load_kv, j_lo, j_diag, mask_fn, robust, row_bias=None, col_bias_fn=None):
    n = len(qs)
    cols = qs[0].shape[0]

    def step(j, carry, diag):
        k, vt = load_kv(j)
        out = []
        for i in range(n):
            m, l, acc = carry[i]
            s = lax.dot_general(k, qs[i], (((1,), (1,)), ((), ())), preferred_element_type=F32)
            if col_bias_fn is not None:
                s = s + col_bias_fn(i, j)
            msk = mask_fn(i, j, diag) if mask_fn is not None else None
            if msk is not None:
                s = jnp.where(msk, s, NEG)
            rmax = jnp.max(s, axis=0, keepdims=True)
            if row_bias is not None:
                rmax = rmax + row_bias[i]
            m_new = jnp.maximum(m, rmax)
            shift = m_new - row_bias[i] if row_bias is not None else m_new
            alpha = jnp.exp2(m - m_new)
            p = jnp.exp2(s - shift)
            if msk is not None and robust:
                p = jnp.where(msk, p, 0.0)
            l = alpha * l + jnp.sum(p, axis=0, keepdims=True)
            acc = alpha * acc + jnp.dot(vt, p.astype(BF16), preferred_element_type=F32)
            out.append((m_new, l, acc))
        return tuple(out)

    dv = load_kv(0)[1].shape[0]
    one = (jnp.full((1, cols), NEG, F32), jnp.zeros((1, cols), F32), jnp.zeros((dv, cols), F32))
    carry = lax.fori_loop(j_lo, j_diag, lambda j, c: step(j, c, False), (one,) * n)
    return [acc / jnp.maximum(l, 1e-30) for _, l, acc in step(j_diag, carry, True)]


def _causal_mask(q0, tq, cols, j, tk):
    kpos = j * tk + lax.broadcasted_iota(jnp.int32, (tk, cols), 0)
    qpos = q0 + lax.broadcasted_iota(jnp.int32, (tk, cols), 1) % tq
    return qpos, kpos


def _chunked_t(x, tk):
    b, s, w = x.shape
    return x.reshape(b, s // tk, tk, w // LANES, LANES).transpose(0, 3, 1, 4, 2)


def _head_pair_kernel(*refs, tq, tk, mode, lam_init):
    if mode == "fox":
        q_ref, k_ref, vt_ref, cq_ref, ck_ref, o_ref = refs
    else:
        q_ref, k_ref, vt_ref, lam_ref, subln_ref, o_ref = refs
    q0 = pl.program_id(2) * tq
    q = q_ref[0]
    lane = lax.broadcasted_iota(jnp.int32, (tq, LANES), 1)
    zero = jnp.zeros_like(q)

    def load_kv(j):
        st = pl.multiple_of(j * tk, tk)
        return k_ref[0, pl.ds(st, tk), :], vt_ref[0, 0, j]

    def mask_fn(i, j, diag):
        if not diag:
            return None
        qpos, kpos = _causal_mask(q0, tq, tq, j, tk)
        return kpos <= qpos

    qs = [jnp.where((lane >= 64 * i) & (lane < 64 * (i + 1)), q, zero) for i in range(2)]
    if mode == "fox":
        outs = _flash_streams(qs, load_kv, 0, q0 // tk, mask_fn, False,
                              row_bias=[cq_ref[0, 0, i:i + 1, :] for i in range(2)],
                              col_bias_fn=lambda i, j: -ck_ref[0, 0, j][:, i:i + 1])
        sub = lax.broadcasted_iota(jnp.int32, outs[0].shape, 0)
        ot = jnp.where(sub < 64, outs[0], outs[1])
    else:
        outs = _flash_streams(qs, load_kv, 0, q0 // tk, mask_fn, False)
        ot = outs[0] - lam_ref[...] * outs[1]
        ms = jnp.mean(ot * ot, axis=0, keepdims=True)
        ot = ot * lax.rsqrt(ms + NORM_EPS) * subln_ref[...] * (1.0 - lam_init)
    o_ref[0] = ot.T.astype(o_ref.dtype)


ATT_TQ = 512
ATT_TK = 1024
ATT_TQ_NSA = 256


def fox_prompt_attention(q, k, v, c):
    b, s, d = q.shape
    tq, tk = min(ATT_TQ, s), min(ATT_TK, s)
    hp = d // LANES
    cq = c.reshape(b, s, hp, 2).transpose(0, 2, 3, 1)
    ck = c.reshape(b, s // tk, tk, hp, 2).transpose(0, 3, 1, 2, 4)
    return pl.pallas_call(
        functools.partial(_head_pair_kernel, tq=tq, tk=tk, mode="fox", lam_init=0.0),
        grid=(b, hp, s // tq),
        in_specs=[pl.BlockSpec((1, tq, LANES), lambda bi, h, qi: (bi, qi, h)),
                  pl.BlockSpec((1, s, LANES), lambda bi, h, qi: (bi, 0, h)),
                  pl.BlockSpec((1, 1, s // tk, LANES, tk), lambda bi, h, qi: (bi, h, 0, 0, 0)),
                  pl.BlockSpec((1, 1, 2, tq), lambda bi, h, qi: (bi, h, 0, qi)),
                  pl.BlockSpec((1, 1, s // tk, tk, 2), lambda bi, h, qi: (bi, h, 0, 0, 0))],
        out_specs=pl.BlockSpec((1, tq, LANES), lambda bi, h, qi: (bi, qi, h)),
        out_shape=jax.ShapeDtypeStruct((b, s, d), BF16),
        compiler_params=_ARB(3), name="fox_prompt_attention",
    )(q, k, _chunked_t(v, tk), cq, ck)


def diff_prompt_attention(q, k, v, lam, lam_init, subln):
    b, s, d = q.shape
    tq, tk = min(ATT_TQ, s), min(ATT_TK, s)
    hp = d // LANES
    lam_vec = jnp.broadcast_to(lam.astype(F32).reshape(1, 1), (1, tq))
    return pl.pallas_call(
        functools.partial(_head_pair_kernel, tq=tq, tk=tk, mode="diff", lam_init=lam_init),
        grid=(b, hp, s // tq),
        in_specs=[pl.BlockSpec((1, tq, LANES), lambda bi, h, qi: (bi, qi, h)),
                  pl.BlockSpec((1, s, LANES), lambda bi, h, qi: (bi, 0, h)),
                  pl.BlockSpec((1, 1, s // tk, LANES, tk), lambda bi, h, qi: (bi, h, 0, 0, 0)),
                  pl.BlockSpec((1, tq), lambda bi, h, qi: (0, 0)),
                  pl.BlockSpec((LANES, 1), lambda bi, h, qi: (0, 0))],
        out_specs=pl.BlockSpec((1, tq, LANES), lambda bi, h, qi: (bi, qi, h)),
        out_shape=jax.ShapeDtypeStruct((b, s, d), BF16),
        compiler_params=_ARB(3), name="diff_prompt_attention",
    )(q, k, _chunked_t(v, tk), lam_vec, subln.reshape(LANES, 1))


def _compress_core(get_phase, pe_ref, w1_ref, w2_ref, nc):
    lo = jnp.zeros((nc, 4 * HEAD_DIM), F32)
    hi = jnp.zeros((nc, 4 * HEAD_DIM), F32)
    for l in range(NSA_CMP_STRIDE):
        ph = get_phase(l)
        lo = lo + jnp.dot((ph + pe_ref[l:l + 1, :]).astype(BF16), w1_ref[l],
                          preferred_element_type=F32)
        hi = hi + jnp.dot((ph + pe_ref[l + NSA_CMP_STRIDE:l + NSA_CMP_STRIDE + 1, :]).astype(BF16),
                          w1_ref[l + NSA_CMP_STRIDE], preferred_element_type=F32)
    pre = lo + pltpu.roll(hi, nc - 1, 0)
    act = jax.nn.gelu(pre)
    return jnp.dot(act.astype(BF16), w2_ref[...], preferred_element_type=F32)


def _nsa_compress_kernel(rows_ref, pe_ref, w1_ref, w2_ref, o_ref, *, nc):
    out = _compress_core(lambda l: rows_ref[0, l], pe_ref, w1_ref, w2_ref, nc)
    for g in range(NSA_GROUPS):
        o_ref[0, g] = out[:, g * HEAD_DIM:(g + 1) * HEAD_DIM].astype(o_ref.dtype)


def _block_diag4(w):
    eye = jnp.eye(NSA_GROUPS, dtype=w.dtype)
    out = jnp.einsum("gh,...ij->...gihj", eye, w)
    return out.reshape(w.shape[:-2] + (NSA_GROUPS * HEAD_DIM, NSA_GROUPS * HEAD_DIM))


def nsa_compress_weights(pe, w1, w2):
    pe4 = jnp.tile(pe, (1, NSA_GROUPS))
    w1bd = _block_diag4(w1.reshape(NSA_CMP_LEN, HEAD_DIM, HEAD_DIM)).astype(BF16)
    w2bd = _block_diag4(w2).astype(BF16)
    return pe4, w1bd, w2bd


def nsa_compress_prompt(rows, cw):
    b, s, d = rows.shape
    nc = s // NSA_CMP_STRIDE
    pe4, w1bd, w2bd = cw
    return pl.pallas_call(
        functools.partial(_nsa_compress_kernel, nc=nc),
        grid=(b,),
        in_specs=[pl.BlockSpec((1, NSA_CMP_STRIDE, nc, d), lambda bi: (bi, 0, 0, 0)),
                  pl.BlockSpec(pe4.shape, lambda bi: (0, 0)),
                  pl.BlockSpec(w1bd.shape, lambda bi: (0, 0, 0)),
                  pl.BlockSpec(w2bd.shape, lambda bi: (0, 0))],
        out_specs=pl.BlockSpec((1, NSA_GROUPS, nc, HEAD_DIM), lambda bi: (bi, 0, 0, 0)),
        out_shape=jax.ShapeDtypeStruct((b, NSA_GROUPS, nc, HEAD_DIM), BF16),
        compiler_params=_ARB(1), name="nsa_compress",
    )(rows.reshape(b, nc, NSA_CMP_STRIDE, d).transpose(0, 2, 1, 3), pe4, w1bd, w2bd)


def _select_blocks(imp, pos, ns):
    blk = lax.broadcasted_iota(jnp.int32, imp.shape, 1)
    cur = pos // NSA_SEL_LEN
    valid = blk * NSA_SEL_LEN <= pos
    forced = (blk == 0) | ((blk <= cur) & (blk > cur - NSA_N_LOCAL))
    score = jnp.where(forced, jnp.inf, jnp.where(valid, imp, -jnp.inf))
    rank = jnp.zeros(imp.shape, jnp.int32)
    for i in range(ns):
        ci = score[:, i:i + 1]
        ahead = (ci > score) | ((ci == score) & (blk > i))
        rank = rank + ahead.astype(jnp.int32)
    return rank < NSA_TOP_N


def _select_blocks_t(imp, pos, ns):
    blk = lax.broadcasted_iota(jnp.int32, imp.shape, 0)
    cur = pos // NSA_SEL_LEN
    valid = blk * NSA_SEL_LEN <= pos
    forced = (blk == 0) | ((blk <= cur) & (blk > cur - NSA_N_LOCAL))
    score = jnp.where(forced, jnp.inf, jnp.where(valid, imp, -jnp.inf))
    rank = jnp.zeros(imp.shape, jnp.int32)
    for i in range(ns):
        ci = score[i:i + 1, :]
        ahead = (ci > score) | ((ci == score) & (blk > i))
        rank = rank + ahead.astype(jnp.int32)
    return rank < NSA_TOP_N


def _nsa_cmp_kernel(q_ref, kc_ref, vc_ref, ov_ref, gate_ref, o_ref, sel_ref, *, tq, nc, ns):
    q0 = pl.program_id(2) * tq
    q = q_ref[0]
    kc, vc = kc_ref[0, 0], vc_ref[0, 0]
    pos = q0 + lax.broadcasted_iota(jnp.int32, (tq, 1), 0)
    c_last = lax.broadcasted_iota(jnp.int32, (1, nc), 1) * NSA_CMP_STRIDE + (NSA_CMP_LEN - 1)
    valid = c_last <= pos
    gate = gate_ref[0, 0]
    psum = jnp.zeros((tq, nc), F32)
    outs = []
    for r in range(NSA_REP):
        qr = q[:, r * HEAD_DIM:(r + 1) * HEAD_DIM]
        s = lax.dot_general(qr, kc, (((1,), (1,)), ((), ())), preferred_element_type=F32)
        s = jnp.where(valid, s, NEG)
        e = jnp.where(valid, jnp.exp2(s - jnp.max(s, axis=-1, keepdims=True)), 0.0)
        p = e / jnp.maximum(jnp.sum(e, axis=-1, keepdims=True), 1e-30)
        outs.append(jnp.dot(p.astype(BF16), vc, preferred_element_type=F32) * gate[:, r:r + 1])
        psum = psum + p
    o_ref[0] = jnp.concatenate(outs, axis=-1)
    imp = lax.dot_general(ov_ref[...], psum, (((1,), (1,)), ((), ())), preferred_element_type=F32,
                          precision=lax.Precision.HIGHEST)
    pos_row = q0 + lax.broadcasted_iota(jnp.int32, (1, tq), 1)
    sel = _select_blocks_t(imp, pos_row, ns)
    sel_ref[0, 0] = jnp.where(sel, 1.0, 0.0).astype(sel_ref.dtype)


def nsa_overlap(nc, ns):
    c0 = np.arange(nc)[:, None] * NSA_CMP_STRIDE
    s0 = np.arange(ns)[None, :] * NSA_SEL_LEN
    ov = np.minimum(c0 + NSA_CMP_LEN, s0 + NSA_SEL_LEN) - np.maximum(c0, s0)
    return (np.clip(ov, 0, None) / NSA_CMP_STRIDE).astype(np.float32)


def nsa_cmp_prompt(q, kc, vc, gate):
    b, s, d = q.shape
    tq = min(ATT_TQ_NSA, s)
    nc, ns = s // NSA_CMP_STRIDE, s // NSA_SEL_LEN
    gw = NSA_REP * HEAD_DIM
    ov = jnp.asarray(nsa_overlap(nc, ns).T)
    return pl.pallas_call(
        functools.partial(_nsa_cmp_kernel, tq=tq, nc=nc, ns=ns),
        grid=(b, NSA_GROUPS, s // tq),
        in_specs=[pl.BlockSpec((1, tq, gw), lambda bi, g, qi: (bi, qi, g)),
                  pl.BlockSpec((1, 1, nc, HEAD_DIM), lambda bi, g, qi: (bi, g, 0, 0)),
                  pl.BlockSpec((1, 1, nc, HEAD_DIM), lambda bi, g, qi: (bi, g, 0, 0)),
                  pl.BlockSpec((ns, nc), lambda bi, g, qi: (0, 0)),
                  pl.BlockSpec((1, 1, tq, NSA_REP), lambda bi, g, qi: (bi, g, qi, 0))],
        out_specs=[pl.BlockSpec((1, tq, gw), lambda bi, g, qi: (bi, qi, g)),
                   pl.BlockSpec((1, 1, ns, tq), lambda bi, g, qi: (bi, g, 0, qi))],
        out_shape=[jax.ShapeDtypeStruct((b, s, d), F32),
                   jax.ShapeDtypeStruct((b, NSA_GROUPS, ns, s), BF16)],
        compiler_params=_ARB(3), name="nsa_cmp_attention",
    )(q, kc, vc, ov, gate)


def _nsa_branch_kernel(*refs, tq, tk, mode):
    if mode == "sel":
        q_ref, k_ref, vt_ref, gate_ref, sel_ref, e_ref, o_ref = refs
    else:
        q_ref, k_ref, vt_ref, gate_ref, o_ref = refs
    q0 = pl.program_id(2) * tq
    q = q_ref[0]
    gate = gate_ref[0, 0]
    lane = lax.broadcasted_iota(jnp.int32, (tq, LANES), 1)
    cols = NSA_REP * tq

    def load_kv(j):
        st = pl.multiple_of(j * tk, tk)
        return k_ref[0, pl.ds(st, tk), :], vt_ref[0, 0, j]

    qs, sms = [], []
    for i in range(2):
        keep = (lane >= 64 * i) & (lane < 64 * (i + 1))
        parts = []
        for r in range(NSA_REP):
            c = 2 * i + r // 2
            x = q[:, c * LANES:(c + 1) * LANES]
            if r % 2 != i:
                x = _swap_halves(x)
            parts.append(jnp.where(keep, x, jnp.zeros_like(x)))
        qs.append(jnp.concatenate(parts, axis=0))
        if mode == "sel":
            penalty = (sel_ref[0, i].astype(F32) - 1.0) * (-NEG)
            sms.append(jnp.concatenate([penalty.astype(BF16)] * NSA_REP, axis=1))
    if mode == "sel":
        def mask_fn(i, j, diag):
            if not diag:
                return None
            qpos, kpos = _causal_mask(q0, tq, cols, j, tk)
            return kpos <= qpos
        bias_fn = lambda i, j: jnp.dot(e_ref[j], sms[i], preferred_element_type=F32)
        j_lo = 0
    else:
        def mask_fn(i, j, diag):
            qpos, kpos = _causal_mask(q0, tq, cols, j, tk)
            return (kpos <= qpos) & (qpos - kpos < NSA_WINDOW)
        bias_fn = None
        j_lo = jnp.maximum(q0 - (NSA_WINDOW - 1), 0) // tk
    outs = _flash_streams(qs, load_kv, j_lo, q0 // tk, mask_fn, True, col_bias_fn=bias_fn)
    pieces = []
    for i in range(2):
        for r in range(NSA_REP):
            hh = NSA_REP * i + r
            pieces.append(outs[i][64 * i:64 * (i + 1), r * tq:(r + 1) * tq] * gate[hh:hh + 1, :])
    o_ref[0] = jnp.concatenate(pieces, axis=0).T


SEL_TK = 512
WIN_TK = 256


def nsa_branch_prompt(q, k, v, gate, sel=None):
    b, s, d = q.shape
    mode = "sel" if sel is not None else "win"
    tq = min(ATT_TQ_NSA, s)
    tk = min(SEL_TK if mode == "sel" else WIN_TK, s)
    ns = s // NSA_SEL_LEN
    qw = 2 * NSA_REP * HEAD_DIM
    args = [q, k, _chunked_t(v, tk), gate]
    in_specs = [pl.BlockSpec((1, tq, qw), lambda bi, gp, qi: (bi, qi, gp)),
                pl.BlockSpec((1, s, LANES), lambda bi, gp, qi: (bi, 0, gp)),
                pl.BlockSpec((1, 1, s // tk, LANES, tk), lambda bi, gp, qi: (bi, gp, 0, 0, 0)),
                pl.BlockSpec((1, 1, 8, tq), lambda bi, gp, qi: (bi, gp, 0, qi))]
    if mode == "sel":
        kb = (np.arange(s) // NSA_SEL_LEN).reshape(s // tk, tk, 1)
        expand = jnp.asarray(kb == np.arange(ns).reshape(1, 1, ns), dtype=BF16)
        args += [sel, expand]
        in_specs += [pl.BlockSpec((1, 2, ns, tq), lambda bi, gp, qi: (bi, gp, 0, qi)),
                     pl.BlockSpec(expand.shape, lambda bi, gp, qi: (0, 0, 0))]
    return pl.pallas_call(
        functools.partial(_nsa_branch_kernel, tq=tq, tk=tk, mode=mode),
        grid=(b, 2, s // tq), in_specs=in_specs,
        out_specs=pl.BlockSpec((1, tq, qw), lambda bi, gp, qi: (bi, qi, gp)),
        out_shape=jax.ShapeDtypeStruct((b, s, d), F32),
        compiler_params=_ARB(3), name="nsa_%s_attention" % mode,
    )(*args)


PAGES_PER_STEP = 8
DEC_HEADS = 16


def _decode_attn_kernel(*refs, n_blk, blk_rows, has_bias, has_self, paged):
    it = iter(refs)
    if paged:
        next(it)
    qt_ref = next(it)
    bias_ref = next(it) if has_bias else None
    if has_self:
        ks_ref, vs_ref = next(it), next(it)
    k_refs = [next(it) for _ in range(n_blk)]
    v_refs = [next(it) for _ in range(n_blk)]
    o_ref = next(it)
    m_sc, l_sc, acc_sc = next(it), next(it), next(it)
    j = pl.program_id(1)

    @pl.when(j == 0)
    def _():
        m_sc[...] = jnp.full(m_sc.shape, NEG, F32)
        l_sc[...] = jnp.zeros(l_sc.shape, F32)
        acc_sc[...] = jnp.zeros(acc_sc.shape, F32)

    qt = qt_ref[0]
    m, l, acc = m_sc[...], l_sc[...], acc_sc[...]
    for i in range(n_blk):
        k = k_refs[i][...].astype(BF16)
        v = v_refs[i][...].astype(BF16)
        s = lax.dot_general(qt, k, (((1,), (1,)), ((), ())), preferred_element_type=F32)
        if has_bias:
            s = s + bias_ref[0, :, i * blk_rows:(i + 1) * blk_rows]
        m_new = jnp.maximum(m, jnp.max(s, axis=-1, keepdims=True))
        alpha = jnp.exp(m - m_new)
        p = jnp.exp(s - m_new)
        l = alpha * l + jnp.sum(p, axis=-1, keepdims=True)
        acc = alpha * acc + jnp.dot(p.astype(BF16), v, preferred_element_type=F32)
        m = m_new
    m_sc[...], l_sc[...], acc_sc[...] = m, l, acc

    @pl.when(j == pl.num_programs(1) - 1)
    def _():
        m, l, acc = m_sc[...], l_sc[...], acc_sc[...]
        if has_self:
            s = jnp.sum(qt.astype(F32) * ks_ref[0], axis=-1, keepdims=True)
            m_new = jnp.maximum(m, s)
            alpha = jnp.exp(m - m_new)
            p = jnp.exp(s - m_new)
            l = alpha * l + p
            acc = alpha * acc + p * vs_ref[0]
        o_ref[0] = acc / jnp.maximum(l, 1e-30)


def decode_attention(qt, k, v, bias=None, k_self=None, v_self=None, page_table=None):
    b, nh, lk = qt.shape
    lv = v.shape[-1]
    paged = page_table is not None
    has_bias, has_self = bias is not None, k_self is not None
    if paged:
        blk_rows, n_blk = k.shape[1], PAGES_PER_STEP
        steps = page_table.shape[1] // n_blk
    else:
        blk_rows, n_blk = min(k.shape[1], 512), 1
        steps = k.shape[1] // blk_rows
    pre = (lambda f: (lambda bi, j, pt: f(bi, j, pt))) if paged else (lambda f: (lambda bi, j: f(bi, j, None)))
    args = [qt]
    in_specs = [pl.BlockSpec((1, nh, lk), pre(lambda bi, j, pt: (bi, 0, 0)))]
    if has_bias:
        args.append(bias)
        in_specs.append(pl.BlockSpec((1, nh, n_blk * blk_rows), pre(lambda bi, j, pt: (bi, 0, j))))
    if has_self:
        args += [k_self, v_self]
        in_specs += [pl.BlockSpec((1, 1, lk), pre(lambda bi, j, pt: (bi, 0, 0))),
                     pl.BlockSpec((1, 1, lv), pre(lambda bi, j, pt: (bi, 0, 0)))]
    for arr, width in ((k, lk), (v, lv)):
        for i in range(n_blk):
            args.append(arr)
            if paged:
                in_specs.append(pl.BlockSpec((None, blk_rows, width),
                                             lambda bi, j, pt, i=i: (pt[bi, j * n_blk + i], 0, 0)))
            else:
                in_specs.append(pl.BlockSpec((None, blk_rows, width), lambda bi, j: (bi, j, 0)))
    out_spec = pl.BlockSpec((1, nh, lv), pre(lambda bi, j, pt: (bi, 0, 0)))
    scratch = [pltpu.VMEM((nh, 1), F32), pltpu.VMEM((nh, 1), F32), pltpu.VMEM((nh, lv), F32)]
    kern = functools.partial(_decode_attn_kernel, n_blk=n_blk, blk_rows=blk_rows,
                             has_bias=has_bias, has_self=has_self, paged=paged)
    out_shape = jax.ShapeDtypeStruct((b, nh, lv), F32)
    if paged:
        return pl.pallas_call(
            kern, out_shape=out_shape,
            grid_spec=pltpu.PrefetchScalarGridSpec(
                num_scalar_prefetch=1, grid=(b, steps), in_specs=in_specs, out_specs=out_spec,
                scratch_shapes=scratch),
            compiler_params=_ARB(2), name="decode_attention_paged",
        )(page_table, *args)
    return pl.pallas_call(
        kern, out_shape=out_shape, grid=(b, steps), in_specs=in_specs, out_specs=out_spec,
        scratch_shapes=scratch, compiler_params=_ARB(2), name="decode_attention",
    )(*args)


def _decode_keylane_kernel(pt_ref, q_ref, bias_ref, ks_ref, vs_ref, *rest, n_blk):
    k_refs, v_refs = rest[:n_blk], rest[n_blk:2 * n_blk]
    o_ref, m_sc, l_sc, acc_sc = rest[2 * n_blk:]
    j = pl.program_id(1)

    @pl.when(j == 0)
    def _():
        m_sc[...] = jnp.full(m_sc.shape, NEG, F32)
        l_sc[...] = jnp.zeros(l_sc.shape, F32)
        acc_sc[...] = jnp.zeros(acc_sc.shape, F32)

    q = q_ref[0]
    qb = jnp.broadcast_to(q, acc_sc.shape)
    m, l, acc = m_sc[...], l_sc[...], acc_sc[...]
    for i in range(n_blk):
        kt = k_refs[i][...][:, None]
        vt = v_refs[i][...][:, None]
        s = jnp.sum(qb * kt, axis=2, keepdims=True) + bias_ref[0, i]
        m_new = jnp.maximum(m, jnp.max(s, axis=-1, keepdims=True))
        alpha = jnp.exp(m - m_new)
        p = jnp.exp(s - m_new)
        l = alpha * l + jnp.sum(p, axis=-1, keepdims=True)
        acc = alpha * acc + p * vt
        m = m_new
    m_sc[...], l_sc[...], acc_sc[...] = m, l, acc

    @pl.when(j == pl.num_programs(1) - 1)
    def _():
        m, l, acc = m_sc[...], l_sc[...], acc_sc[...]
        s = jnp.sum(q * ks_ref[0], axis=2, keepdims=True)
        m_new = jnp.maximum(m, s)
        alpha = jnp.exp(m - m_new)
        p = jnp.exp(s - m_new)
        l = alpha * l + p
        o = alpha * jnp.sum(acc, axis=-1, keepdims=True) + p * vs_ref[0]
        o_ref[0] = o / jnp.maximum(l, 1e-30)


def decode_attention_keylane(q, k_pool, v_pool, bias, k_self, v_self, page_table):
    b, n_kv, rep, dh = q.shape
    n_pages = page_table.shape[1]
    psz = k_pool.shape[-1]
    n_blk = PAGES_PER_STEP
    bias6 = bias.reshape(b, n_kv, n_pages, 1, 1, psz).transpose(0, 2, 1, 3, 4, 5)
    args = [q.reshape(b, n_kv, rep, dh, 1), bias6, k_self.reshape(b, n_kv, 1, dh, 1), v_self.reshape(b, n_kv, 1, dh, 1)]
    in_specs = [pl.BlockSpec((1, n_kv, rep, dh, 1), lambda bi, j, pt: (bi, 0, 0, 0, 0)),
                pl.BlockSpec((1, n_blk, n_kv, 1, 1, psz), lambda bi, j, pt: (bi, j, 0, 0, 0, 0)),
                pl.BlockSpec((1, n_kv, 1, dh, 1), lambda bi, j, pt: (bi, 0, 0, 0, 0)),
                pl.BlockSpec((1, n_kv, 1, dh, 1), lambda bi, j, pt: (bi, 0, 0, 0, 0))]
    for pool in (k_pool, v_pool):
        for i in range(n_blk):
            args.append(pool)
            in_specs.append(pl.BlockSpec((None, n_kv, dh, psz),
                                         lambda bi, j, pt, i=i: (pt[bi, j * n_blk + i], 0, 0, 0)))
    out = pl.pallas_call(
        functools.partial(_decode_keylane_kernel, n_blk=n_blk),
        out_shape=jax.ShapeDtypeStruct((b, n_kv, rep, dh, 1), F32),
        grid_spec=pltpu.PrefetchScalarGridSpec(
            num_scalar_prefetch=1, grid=(b, n_pages // n_blk), in_specs=in_specs,
            out_specs=pl.BlockSpec((1, n_kv, rep, dh, 1), lambda bi, j, pt: (bi, 0, 0, 0, 0)),
            scratch_shapes=[pltpu.VMEM((n_kv, rep, 1, 1), F32), pltpu.VMEM((n_kv, rep, 1, 1), F32),
                            pltpu.VMEM((n_kv, rep, dh, psz), F32)]),
        compiler_params=_ARB(2), name="decode_attention_keylane",
    )(page_table, *args)
    return out.reshape(b, n_kv, rep, dh)


def _decode_keylane_mxu_kernel(pt_ref, q_ref, bias_ref, ks_ref, vs_ref, *rest, n_blk):
    k_refs, v_refs = rest[:n_blk], rest[n_blk:2 * n_blk]
    o_ref, m_sc, l_sc, acc_sc = rest[2 * n_blk:]
    j = pl.program_id(1)

    @pl.when(j == 0)
    def _():
        m_sc[...] = jnp.full(m_sc.shape, NEG, F32)
        l_sc[...] = jnp.zeros(l_sc.shape, F32)
        acc_sc[...] = jnp.zeros(acc_sc.shape, F32)

    q = q_ref[0]
    q16 = q.astype(BF16)
    m, l, acc = m_sc[...], l_sc[...], acc_sc[...]
    group = 8
    for i0 in range(0, n_blk, group):
        idx = range(i0, min(i0 + group, n_blk))
        ss = [lax.dot_general(q16, k_refs[i][...].astype(BF16), (((2,), (1,)), ((0,), (0,))),
                              preferred_element_type=F32) + bias_ref[0, i] for i in idx]
        m_new = m
        for s in ss:
            m_new = jnp.maximum(m_new, jnp.max(s, axis=-1, keepdims=True))
        alpha = jnp.exp(m - m_new)
        l = alpha * l
        acc = alpha * acc
        for i, s in zip(idx, ss):
            p = jnp.exp(s - m_new)
            l = l + jnp.sum(p, axis=-1, keepdims=True)
            acc = acc + lax.dot_general(p.astype(BF16), v_refs[i][...].astype(BF16), (((2,), (2,)), ((0,), (0,))),
                                        preferred_element_type=F32)
        m = m_new
    m_sc[...], l_sc[...], acc_sc[...] = m, l, acc

    @pl.when(j == pl.num_programs(1) - 1)
    def _():
        m, l, acc = m_sc[...], l_sc[...], acc_sc[...]
        s = jnp.sum(q * ks_ref[0], axis=-1, keepdims=True)
        m_new = jnp.maximum(m, s)
        alpha = jnp.exp(m - m_new)
        p = jnp.exp(s - m_new)
        o_ref[0] = (alpha * acc + p * vs_ref[0]) / jnp.maximum(alpha * l + p, 1e-30)


def decode_attention_keylane_mxu(q, k_pool, v_pool, bias, k_self, v_self, page_table):
    b, n_kv, rep, dh = q.shape
    n_pages = page_table.shape[1]
    psz = k_pool.shape[-1]
    rows = -(-rep // 8) * 8
    n_blk = min(4 * PAGES_PER_STEP, n_pages)
    qp = jnp.pad(q, ((0, 0), (0, 0), (0, rows - rep), (0, 0)))
    bias5 = bias.reshape(b, n_kv, n_pages, 1, psz).transpose(0, 2, 1, 3, 4)
    args = [qp, bias5, k_self.reshape(b, n_kv, 1, dh), v_self.reshape(b, n_kv, 1, dh)]
    in_specs = [pl.BlockSpec((1, n_kv, rows, dh), lambda bi, j, pt: (bi, 0, 0, 0)),
                pl.BlockSpec((1, n_blk, n_kv, 1, psz), lambda bi, j, pt: (bi, j, 0, 0, 0)),
                pl.BlockSpec((1, n_kv, 1, dh), lambda bi, j, pt: (bi, 0, 0, 0)),
                pl.BlockSpec((1, n_kv, 1, dh), lambda bi, j, pt: (bi, 0, 0, 0))]
    for pool in (k_pool, v_pool):
        for i in range(n_blk):
            args.append(pool)
            in_specs.append(pl.BlockSpec((None, n_kv, dh, psz),
                                         lambda bi, j, pt, i=i: (pt[bi, j * n_blk + i], 0, 0, 0)))
    out = pl.pallas_call(
        functools.partial(_decode_keylane_mxu_kernel, n_blk=n_blk),
        out_shape=jax.ShapeDtypeStruct((b, n_kv, rows, dh), F32),
        grid_spec=pltpu.PrefetchScalarGridSpec(
            num_scalar_prefetch=1, grid=(b, n_pages // n_blk), in_specs=in_specs,
            out_specs=pl.BlockSpec((1, n_kv, rows, dh), lambda bi, j, pt: (bi, 0, 0, 0)),
            scratch_shapes=[pltpu.VMEM((n_kv, rows, 1), F32), pltpu.VMEM((n_kv, rows, 1), F32),
                            pltpu.VMEM((n_kv, rows, dh), F32)]),
        compiler_params=_ARB(2), name="decode_attention_keylane_mxu",
    )(page_table, *args)
    return out[:, :, :rep]


def _decode_diff_kernel(pt_ref, q_ref, ks_ref, vs_ref, e_ref, *rest, n_blk, n_heads):
    k_refs, v_refs = rest[:n_blk], rest[n_blk:2 * n_blk]
    o_ref, m_sc, l_sc, acc_sc = rest[2 * n_blk:]
    j = pl.program_id(1)

    @pl.when(j == 0)
    def _():
        m_sc[...] = jnp.full(m_sc.shape, NEG, F32)
        l_sc[...] = jnp.zeros(l_sc.shape, F32)
        acc_sc[...] = jnp.zeros(acc_sc.shape, F32)

    q = q_ref[0]
    qb = jnp.broadcast_to(q, q.shape[:2] + (LANES,))
    n_sub = q.shape[0]
    wide = e_ref.shape[1]
    own = (lax.broadcasted_iota(jnp.int32, (n_sub, wide), 1) % n_heads
           == lax.broadcasted_iota(jnp.int32, (n_sub, wide), 0) // (n_sub // n_heads))
    m, l, acc = m_sc[...], l_sc[...], acc_sc[...]
    for i in range(n_blk):
        s = jnp.sum(qb * k_refs[i][...], axis=1)
        m_new = jnp.maximum(m, jnp.max(s, axis=-1, keepdims=True))
        alpha = jnp.exp(m - m_new)
        p = jnp.exp(s - m_new)
        l = alpha * l + jnp.sum(p, axis=-1, keepdims=True)
        spread = jnp.dot(p.astype(BF16), e_ref[...], preferred_element_type=F32)
        spread = jnp.where(own, spread, 0.0).astype(BF16)
        acc = alpha * acc + jnp.dot(spread, v_refs[i][...].astype(BF16), preferred_element_type=F32)
        m = m_new
    m_sc[...], l_sc[...], acc_sc[...] = m, l, acc

    @pl.when(j == pl.num_programs(1) - 1)
    def _():
        m, l, acc = m_sc[...], l_sc[...], acc_sc[...]
        s = jnp.sum(q * ks_ref[0], axis=1)
        m_new = jnp.maximum(m, s)
        alpha = jnp.exp(m - m_new)
        p = jnp.exp(s - m_new)
        o_ref[0] = (alpha * acc + p * vs_ref[0]) / jnp.maximum(alpha * l + p, 1e-30)


def decode_attention_diff(q, k_pool, v_pool, k_self, v_self, page_table):
    b, n_sub, dh = q.shape
    n_pages = page_table.shape[1]
    psz, n_heads, dv = v_pool.shape[1:]
    n_blk = PAGES_PER_STEP
    v_rows = v_pool.reshape(v_pool.shape[0], psz * n_heads, dv)
    expand = jnp.asarray(np.arange(psz)[:, None] == np.arange(psz * n_heads)[None, :] // n_heads, dtype=BF16)
    args = [q.reshape(b, n_sub, dh, 1), k_self.reshape(b, n_sub, dh, 1), v_self, expand]
    in_specs = [pl.BlockSpec((1, n_sub, dh, 1), lambda bi, j, pt: (bi, 0, 0, 0)),
                pl.BlockSpec((1, n_sub, dh, 1), lambda bi, j, pt: (bi, 0, 0, 0)),
                pl.BlockSpec((1, n_sub, dv), lambda bi, j, pt: (bi, 0, 0)),
                pl.BlockSpec(expand.shape, lambda bi, j, pt: (0, 0))]
    for i in range(n_blk):
        args.append(k_pool)
        in_specs.append(pl.BlockSpec((None, n_sub, dh, psz), lambda bi, j, pt, i=i: (pt[bi, j * n_blk + i], 0, 0, 0)))
    for i in range(n_blk):
        args.append(v_rows)
        in_specs.append(pl.BlockSpec((None, psz * n_heads, dv), lambda bi, j, pt, i=i: (pt[bi, j * n_blk + i], 0, 0)))
    return pl.pallas_call(
        functools.partial(_decode_diff_kernel, n_blk=n_blk, n_heads=n_heads),
        out_shape=jax.ShapeDtypeStruct((b, n_sub, dv), F32),
        grid_spec=pltpu.PrefetchScalarGridSpec(
            num_scalar_prefetch=1, grid=(b, n_pages // n_blk), in_specs=in_specs,
            out_specs=pl.BlockSpec((1, n_sub, dv), lambda bi, j, pt: (bi, 0, 0)),
            scratch_shapes=[pltpu.VMEM((n_sub, 1), F32), pltpu.VMEM((n_sub, 1), F32),
                            pltpu.VMEM((n_sub, dv), F32)]),
        compiler_params=_ARB(2), name="decode_attention_diff",
    )(page_table, *args)


def _paged_cumsum_kernel(pt_ref, tri_ref, *rest, n_blk):
    x_refs = rest[:n_blk]
    c_ref, tot_ref, carry_sc = rest[n_blk:]
    j = pl.program_id(1)

    @pl.when(j == 0)
    def _():
        carry_sc[...] = jnp.zeros(carry_sc.shape, F32)

    carry = carry_sc[...]
    psz = tri_ref.shape[0]
    nh = carry.shape[0]
    x = jnp.concatenate([r[...] for r in x_refs], axis=0)
    pre_all = jnp.dot(x, tri_ref[...], preferred_element_type=F32, precision=lax.Precision.HIGHEST)
    for i in range(n_blk):
        pre = pre_all[i * nh:(i + 1) * nh]
        c_ref[0, :, i * psz:(i + 1) * psz] = pre + carry
        carry = carry + pre[:, psz - 1:psz]
    carry_sc[...] = carry
    tot_ref[0] = jnp.broadcast_to(carry, tot_ref.shape[1:])


def paged_cumsum(pool_t, page_table):
    b, n_pages = page_table.shape
    _, nh, psz = pool_t.shape
    n_blk = min(2 * PAGES_PER_STEP, n_pages)
    tri = jnp.asarray(np.arange(psz)[:, None] <= np.arange(psz)[None, :], dtype=F32)
    in_specs = [pl.BlockSpec(tri.shape, lambda bi, j, pt: (0, 0))]
    in_specs += [pl.BlockSpec((None, nh, psz), lambda bi, j, pt, i=i: (pt[bi, j * n_blk + i], 0, 0))
                 for i in range(n_blk)]
    c, tot = pl.pallas_call(
        functools.partial(_paged_cumsum_kernel, n_blk=n_blk),
        out_shape=[jax.ShapeDtypeStruct((b, nh, n_pages * psz), F32), jax.ShapeDtypeStruct((b, nh, LANES), F32)],
        grid_spec=pltpu.PrefetchScalarGridSpec(
            num_scalar_prefetch=1, grid=(b, n_pages // n_blk), in_specs=in_specs,
            out_specs=[pl.BlockSpec((1, nh, n_blk * psz), lambda bi, j, pt: (bi, 0, j)),
                       pl.BlockSpec((1, nh, LANES), lambda bi, j, pt: (bi, 0, 0))],
            scratch_shapes=[pltpu.VMEM((nh, 1), F32)]),
        compiler_params=_ARB(2), name="paged_cumsum",
    )(page_table, tri, *([pool_t] * n_blk))
    return c, tot[:, :, 0]


def _keys_on_lanes(pool):
    return pool.transpose(0, 2, 3, 1)


def _spread_heads(q, width):
    b = q.shape[0]
    n_kv = width // HEAD_DIM
    owner = (np.arange(DEC_HEADS) * n_kv) // DEC_HEADS
    onehot = jnp.asarray(owner[:, None] == np.arange(n_kv)[None, :], dtype=F32)
    out = q[:, :, None, :] * onehot[None, :, :, None]
    return (out * SCALE).reshape(b, DEC_HEADS, width).astype(BF16)


def _own_lanes(o, width_per_head):
    b, nh, lv = o.shape
    n_kv = lv // width_per_head
    owner = (np.arange(nh) * n_kv) // nh
    o = o.reshape(b, nh, n_kv, width_per_head)
    return jnp.take_along_axis(o, jnp.asarray(owner).reshape(1, nh, 1, 1), axis=2)[:, :, 0]


def _diff_finish_kernel(o0_ref, o1_ref, lam_ref, subln_ref, o_ref, *, lam_init):
    o = o0_ref[...] - lam_ref[...] * o1_ref[...]
    o_ref[...] = _rms(o, subln_ref[...]) * (1.0 - lam_init)


def diff_finish(o0, o1, lam, lam_init, subln):
    m, w = o0.shape
    lam_vec = jnp.broadcast_to(lam.astype(F32).reshape(1, 1), (1, w))
    full = lambda shape: pl.BlockSpec(shape, lambda i: (0, 0))
    return pl.pallas_call(
        functools.partial(_diff_finish_kernel, lam_init=lam_init),
        grid=(1,), in_specs=[full((m, w)), full((m, w)), full((1, w)), full((1, w))],
        out_specs=full((m, w)), out_shape=jax.ShapeDtypeStruct((m, w), F32),
        compiler_params=_ARB(1), name="diff_finish",
    )(o0, o1, lam_vec, subln.reshape(1, w))


def _nsa_cmp_decode_kernel(*refs, n_blk, ncp, ns, past):
    it = iter(refs)
    next(it)
    qt_ref = next(it)
    pages = [[next(it) for _ in range(n_blk)] for _ in range(2)]
    pek_ref, w1k_ref, w2k_ref, pev_ref, w1v_ref, w2v_ref, ov_ref = (next(it) for _ in range(7))
    o_ref, sel_ref = next(it), next(it)
    ph_refs = [next(it), next(it)]
    stage_ref = next(it)
    j = pl.program_id(1)
    for t in range(2):
        for half in range(2):
            for i in range(n_blk):
                pair = jnp.concatenate([pages[t][i][2 * half], pages[t][i][2 * half + 1]], axis=0)
                stage_ref[...] = pair.T
                c0 = pl.multiple_of((j * n_blk + i) * 8, 8)
                for l in range(NSA_CMP_STRIDE):
                    ph_refs[t][l, pl.ds(c0, 8), half * LANES:(half + 1) * LANES] = \
                        stage_ref[pl.ds(l, 8, stride=NSA_CMP_STRIDE), :]

    @pl.when(j == pl.num_programs(1) - 1)
    def _():
        kc = _compress_core(lambda l: ph_refs[0][l], pek_ref, w1k_ref, w2k_ref, ncp)
        vc = _compress_core(lambda l: ph_refs[1][l], pev_ref, w1v_ref, w2v_ref, ncp)
        qt = qt_ref[0]
        s = lax.dot_general(qt, kc.astype(BF16), (((1,), (1,)), ((), ())), preferred_element_type=F32)
        valid = lax.broadcasted_iota(jnp.int32, s.shape, 1) < ncp - 1
        s = jnp.where(valid, s, NEG)
        e = jnp.where(valid, jnp.exp(s - jnp.max(s, axis=-1, keepdims=True)), 0.0)
        p = e / jnp.maximum(jnp.sum(e, axis=-1, keepdims=True), 1e-30)
        o_ref[0] = jnp.dot(p.astype(BF16), vc.astype(BF16), preferred_element_type=F32)
        grp = (lax.broadcasted_iota(jnp.int32, (8, DEC_HEADS), 1) // NSA_REP
               == lax.broadcasted_iota(jnp.int32, (8, DEC_HEADS), 0)).astype(F32)
        psum = jnp.dot(grp, p, preferred_element_type=F32, precision=lax.Precision.HIGHEST)
        imp = jnp.dot(psum, ov_ref[...], preferred_element_type=F32, precision=lax.Precision.HIGHEST)
        sel = _select_blocks(imp, jnp.full((8, 1), past, jnp.int32), ns)
        sel_ref[0] = jnp.where(sel, 1.0, 0.0)


def nsa_cmp_decode(qt, pool_k, pool_v, page_table, cw_k, cw_v):
    b = qt.shape[0]
    n_pages = page_table.shape[1]
    past = n_pages * pool_k.shape[-1]
    n_blk = PAGES_PER_STEP
    ncp = past // NSA_CMP_STRIDE
    ns = past // NSA_SEL_LEN + 1
    nsp = -(-ns // LANES) * LANES
    ov = jnp.asarray(np.pad(nsa_overlap(ncp, ns), ((0, 0), (0, nsp - ns))))
    args = [qt]
    in_specs = [pl.BlockSpec((1, DEC_HEADS, 4 * HEAD_DIM), lambda bi, j, pt: (bi, 0, 0))]
    for pool in (pool_k, pool_v):
        for i in range(n_blk):
            args.append(pool)
            in_specs.append(pl.BlockSpec((None,) + pool.shape[1:],
                                         lambda bi, j, pt, i=i: (pt[bi, j * n_blk + i], 0, 0, 0)))
    for a in tuple(cw_k) + tuple(cw_v) + (ov,):
        args.append(a)
        in_specs.append(pl.BlockSpec(a.shape, lambda bi, j, pt, nd=a.ndim: (0,) * nd))
    return pl.pallas_call(
        functools.partial(_nsa_cmp_decode_kernel, n_blk=n_blk, ncp=ncp, ns=ns, past=past),
        out_shape=[jax.ShapeDtypeStruct((b, DEC_HEADS, 4 * HEAD_DIM), F32),
                   jax.ShapeDtypeStruct((b, 8, nsp), F32)],
        grid_spec=pltpu.PrefetchScalarGridSpec(
            num_scalar_prefetch=1, grid=(b, n_pages // n_blk), in_specs=in_specs,
            out_specs=[pl.BlockSpec((1, DEC_HEADS, 4 * HEAD_DIM), lambda bi, j, pt: (bi, 0, 0)),
                       pl.BlockSpec((1, 8, nsp), lambda bi, j, pt: (bi, 0, 0))],
            scratch_shapes=[pltpu.VMEM((NSA_CMP_STRIDE, ncp, 4 * HEAD_DIM), F32)] * 2
                           + [pltpu.VMEM((pool_k.shape[-1], LANES), F32)]),
        compiler_params=_ARB(2), name="nsa_cmp_decode",
    )(page_table, *args)


def _pad_cols(w, n):
    return jnp.pad(w, ((0, 0), (0, n - w.shape[1])))


def _pad_vec(v, n):
    return jnp.pad(v, (0, n - v.shape[0]))


def fox_mixer(hp, hs, g, cache_k, cache_v, cache_logf, page_table, w_in, b_f, w_o):
    bp, sp, d = hp.shape
    bs, ds, _ = hs.shape
    past = page_table.shape[1] * cache_k.shape[1]
    hd = FOX_HEADS * HEAD_DIM
    wb = w_in.astype(BF16)
    ws = [wb[:, :hd], wb[:, hd:2 * hd], wb[:, 2 * hd:3 * hd], _pad_cols(wb[:, 3 * hd:], LANES)]
    biases = [None, None, None, _pad_vec(b_f, LANES)]
    wo = w_o.astype(BF16)
    hp2 = hp.reshape(bp * sp, d)
    hs2 = hs.reshape(bs * ds, d)
    cfgs = [ProjCfg(f32=False, bf16=True, scale=Q_SCALE_LOG2), ProjCfg(bf16=True), ProjCfg(bf16=True),
            ProjCfg(act="logsig")]
    q16, k, k16, v, v16, lf = norm_proj(hp2, g, ws, cfgs, biases)
    logf = lf[:, :FOX_HEADS].reshape(bp, sp, FOX_HEADS)
    c = jnp.cumsum(logf, axis=1)
    o_p = fox_prompt_attention(q16.reshape(bp, sp, hd), k16.reshape(bp, sp, hd), v16.reshape(bp, sp, hd), c * LOG2E)
    hp_new = out_proj([o_p.reshape(bp * sp, hd)], wo, hp2).reshape(bp, sp, d)
    sh = (bp, sp, FOX_HEADS, HEAD_DIM)
    k, v = k.reshape(sh), v.reshape(sh)
    cfgs = [ProjCfg(), ProjCfg(), ProjCfg(), ProjCfg(act="logsig")]
    qs, ks, vs, lfs = norm_proj(hs2, g, ws, cfgs, biases)
    logfs = lfs[:, :FOX_HEADS].reshape(bs, ds, FOX_HEADS)
    c_past, c_tot = paged_cumsum(cache_logf.transpose(0, 2, 1), page_table)
    bias = (c_tot + logfs[:, 0])[:, :, None] - c_past
    hsh = (bs, FOX_HEADS, HEAD_DIM)
    o_s = decode_attention_keylane((qs * SCALE).reshape(bs, FOX_HEADS, 1, HEAD_DIM), _keys_on_lanes(cache_k),
                                   _keys_on_lanes(cache_v), bias, ks.reshape(hsh), vs.reshape(hsh), page_table)
    hs_new = out_proj([o_s.reshape(bs * ds, D_MODEL)], wo, hs2).reshape(bs, ds, d)
    shs = (bs, ds, FOX_HEADS, HEAD_DIM)
    ks, vs = ks.reshape(shs), vs.reshape(shs)
    return hp_new, hs_new, (k, v, logf, ks, vs, logfs)


def mlp_both(hp, hs, g, wu, wd):
    wu = wu.astype(BF16)
    wd = wd.astype(BF16)
    yp = mlp(hp.reshape(-1, D_MODEL), g, wu, wd).reshape(hp.shape)
    ys = mlp(hs.reshape(-1, D_MODEL), g, wu, wd).reshape(hs.shape)
    return yp, ys


def nsa_mixer(hp, hs, gn, cache_cmp_k, cache_cmp_v, cache_sel_k, cache_sel_v, state_win_k, state_win_v,
              page_table, w_in, b_gate, pe_k, w1_k, w2_k, pe_v, w1_v, w2_v, w_o):
    bp, sp, d = hp.shape
    bs, ds, _ = hs.shape
    past = page_table.shape[1] * cache_cmp_k.shape[1]
    wb_ = state_win_k.shape[1]
    hq = NSA_HEADS * HEAD_DIM
    kvd = NSA_GROUPS * HEAD_DIM
    wbf = w_in.astype(BF16)
    ws = [wbf[:, :hq]] + [wbf[:, hq + i * kvd:hq + (i + 1) * kvd] for i in range(6)] \
        + [_pad_cols(wbf[:, hq + 6 * kvd:], LANES)]
    biases = [None] * 7 + [_pad_vec(b_gate, LANES)]
    wo = w_o.astype(BF16)
    hp2 = hp.reshape(bp * sp, d)
    hs2 = hs.reshape(bs * ds, d)
    cfgs = [ProjCfg(rope=True, f32=False, bf16=True, scale=Q_SCALE_LOG2),
            ProjCfg(rope=True), ProjCfg(),
            ProjCfg(rope=True, bf16=True), ProjCfg(bf16=True),
            ProjCfg(rope=True, bf16=True), ProjCfg(bf16=True),
            ProjCfg(act="sigmoid")]
    tabs = rope_tables(jnp.arange(sp, dtype=jnp.int32))
    (q16, kc, vc, ksl, ksl16, vsl, vsl16, kw, kw16, vw, vw16, gt) = norm_proj(hp2, gn, ws, cfgs, biases, tabs)
    r3 = lambda a: a.reshape(bp, sp, a.shape[-1])
    q16 = r3(q16)
    gates = gt[:, :3 * NSA_HEADS].reshape(bp, sp, 3, NSA_HEADS)
    g_cmp = gates[:, :, 0].reshape(bp, sp, NSA_GROUPS, NSA_REP).transpose(0, 2, 1, 3)
    g_sel = gates[:, :, 1].reshape(bp, sp, 2, 8).transpose(0, 2, 3, 1)
    g_win = gates[:, :, 2].reshape(bp, sp, 2, 8).transpose(0, 2, 3, 1)
    kcc = nsa_compress_prompt(r3(kc), nsa_compress_weights(pe_k, w1_k, w2_k))
    vcc = nsa_compress_prompt(r3(vc), nsa_compress_weights(pe_v, w1_v, w2_v))
    o_cmp, sel = nsa_cmp_prompt(q16, kcc, vcc, g_cmp)
    o_sel = nsa_branch_prompt(q16, r3(ksl16), r3(vsl16), g_sel, sel)
    o_win = nsa_branch_prompt(q16, r3(kw16), r3(vw16), g_win)
    hp_new = out_proj([o_cmp.reshape(bp * sp, d), o_sel.reshape(bp * sp, d), o_win.reshape(bp * sp, d)],
                      wo, hp2).reshape(bp, sp, d)
    r4 = lambda a: a.reshape(bp, sp, NSA_GROUPS, HEAD_DIM)
    kc, vc, ksl, vsl, kw, vw = r4(kc), r4(vc), r4(ksl), r4(vsl), r4(kw), r4(vw)
    cfgs = [ProjCfg(rope=True), ProjCfg(rope=True), ProjCfg(), ProjCfg(rope=True), ProjCfg(),
            ProjCfg(rope=True), ProjCfg(), ProjCfg(act="sigmoid")]
    tabs_s = rope_tables(jnp.full((bs * ds,), past, jnp.int32))
    qs, kc_s, vc_s, ksl_s, vsl_s, kw_s, vw_s, gt_s = norm_proj(hs2, gn, ws, cfgs, biases, tabs_s)
    r4s = lambda a: a.reshape(bs, ds, NSA_GROUPS, HEAD_DIM)
    kc_s, vc_s, ksl_s, vsl_s, kw_s, vw_s = r4s(kc_s), r4s(vc_s), r4s(ksl_s), r4s(vsl_s), r4s(kw_s), r4s(vw_s)
    gates_s = gt_s[:, :3 * NSA_HEADS].reshape(bs, ds, 3, NSA_HEADS)
    kvd4 = NSA_GROUPS * HEAD_DIM
    row = lambda a: a.reshape(bs, 1, kvd4)
    qt = _spread_heads(qs.reshape(bs, NSA_HEADS, HEAD_DIM), kvd4)
    o_cmp_full, selm = nsa_cmp_decode(qt, _keys_on_lanes(cache_cmp_k), _keys_on_lanes(cache_cmp_v), page_table,
                                      nsa_compress_weights(pe_k, w1_k, w2_k), nsa_compress_weights(pe_v, w1_v, w2_v))
    n_past_blk = past // NSA_SEL_LEN
    sel_keys = jnp.repeat(selm[:, :NSA_GROUPS, :n_past_blk], NSA_SEL_LEN, axis=2)
    bias_sel = jnp.where(sel_keys > 0.5, 0.0, NEG)
    gsh = (bs, NSA_GROUPS, HEAD_DIM)
    o_sel = decode_attention_keylane_mxu((qs * SCALE).reshape(bs, NSA_GROUPS, NSA_REP, HEAD_DIM), _keys_on_lanes(cache_sel_k),
                                         _keys_on_lanes(cache_sel_v), bias_sel, ksl_s.reshape(gsh), vsl_s.reshape(gsh),
                                         page_table).reshape(bs, NSA_HEADS, HEAD_DIM)
    in_win = jnp.arange(wb_) > wb_ - NSA_WINDOW
    bias_win = jnp.broadcast_to(jnp.where(in_win, 0.0, NEG)[None, None, :], (bs, NSA_HEADS, wb_)).astype(F32)
    o_win_full = decode_attention(qt, state_win_k.reshape(bs, wb_, kvd4), state_win_v.reshape(bs, wb_, kvd4),
                                  bias_win, row(kw_s), row(vw_s))
    gs = gates_s.reshape(bs, 3, NSA_HEADS, 1)
    o_s = (gs[:, 0] * _own_lanes(o_cmp_full, HEAD_DIM) + gs[:, 1] * o_sel
           + gs[:, 2] * _own_lanes(o_win_full, HEAD_DIM))
    hs_new = out_proj([o_s.reshape(bs * ds, d)], wo, hs2).reshape(bs, ds, d)
    kw_all = jnp.concatenate([state_win_k, kw_s], axis=1)
    vw_all = jnp.concatenate([state_win_v, vw_s], axis=1)
    keep = min(NSA_WINDOW, sp)
    return hp_new, hs_new, (kc, vc, ksl, vsl, kw[:, sp - keep:], vw[:, sp - keep:],
                            kc_s, vc_s, ksl_s, vsl_s, kw_all[:, ds:], vw_all[:, ds:])


def diff_mixer(hp, hs, gn, cache_k, cache_v, page_table, w_in, lq1, lk1, lq2, lk2, subln, w_o, layer_idx):
    lam_init = 0.8 - 0.6 * math.exp(-0.3 * layer_idx)
    lam = (jnp.exp(jnp.sum(lq1.astype(F32) * lk1.astype(F32)))
           - jnp.exp(jnp.sum(lq2.astype(F32) * lk2.astype(F32))) + lam_init)
    bp, sp, d = hp.shape
    bs, ds, _ = hs.shape
    past = page_table.shape[1] * cache_k.shape[1]
    wbf = w_in.astype(BF16)
    ws = [wbf[:, :d], wbf[:, d:2 * d], wbf[:, 2 * d:]]
    wo = w_o.astype(BF16)
    hp2 = hp.reshape(bp * sp, d)
    hs2 = hs.reshape(bs * ds, d)
    cfgs = [ProjCfg(rope=True, f32=False, bf16=True, scale=Q_SCALE_LOG2), ProjCfg(rope=True, bf16=True),
            ProjCfg(bf16=True)]
    tabs = rope_tables(jnp.arange(sp, dtype=jnp.int32))
    q16, k, k16, v, v16 = norm_proj(hp2, gn, ws, cfgs, None, tabs)
    r3 = lambda a: a.reshape(bp, sp, d)
    o_p = diff_prompt_attention(r3(q16), r3(k16), r3(v16), lam, lam_init, subln)
    hp_new = out_proj([o_p.reshape(bp * sp, d)], wo, hp2).reshape(bp, sp, d)
    k = k.reshape(bp, sp, 2 * DIFF_HEADS, HEAD_DIM)
    v = v.reshape(bp, sp, DIFF_HEADS, 2 * HEAD_DIM)
    cfgs = [ProjCfg(rope=True), ProjCfg(rope=True), ProjCfg()]
    tabs_s = rope_tables(jnp.full((bs * ds,), past, jnp.int32))
    qs, ks, vs = norm_proj(hs2, gn, ws, cfgs, None, tabs_s)
    o_sub = decode_attention_diff((qs * SCALE).reshape(bs, 2 * DIFF_HEADS, HEAD_DIM), _keys_on_lanes(cache_k), cache_v,
                                  ks.reshape(bs, 2 * DIFF_HEADS, HEAD_DIM),
                                  jnp.repeat(vs.reshape(bs, DIFF_HEADS, 2 * HEAD_DIM), 2, axis=1), page_table)
    o_s = diff_finish(o_sub[:, 0::2].reshape(bs * DIFF_HEADS, 2 * HEAD_DIM),
                      o_sub[:, 1::2].reshape(bs * DIFF_HEADS, 2 * HEAD_DIM), lam, lam_init, subln)
    hs_new = out_proj([o_s.reshape(bs * ds, D_MODEL)], wo, hs2).reshape(bs, ds, d)
    ks = ks.reshape(bs, ds, 2 * DIFF_HEADS, HEAD_DIM)
    vs = vs.reshape(bs, ds, DIFF_HEADS, 2 * HEAD_DIM)
    return hp_new, hs_new, (k, v, ks, vs)


def kernel(x_prompt, x_sample, cache_l0_k, cache_l0_v, cache_l0_logf, cache_l1_cmp_k, cache_l1_cmp_v, cache_l1_sel_k, cache_l1_sel_v, state_l1_win_k, state_l1_win_v, cache_l2_k, cache_l2_v, cache_l3_k, cache_l3_v, cache_l3_logf, page_table, l0_norm_mix, l0_fox_w_in, l0_fox_b_f, l0_fox_w_o, l0_norm_mlp, l0_mlp_up, l0_mlp_down, l1_norm_mix, l1_nsa_w_in, l1_nsa_b_gate, l1_nsa_pe_k, l1_nsa_w1_k, l1_nsa_w2_k, l1_nsa_pe_v, l1_nsa_w1_v, l1_nsa_w2_v, l1_nsa_w_o, l1_norm_mlp, l1_mlp_up, l1_mlp_down, l2_norm_mix, l2_diff_w_in, l2_diff_lq1, l2_diff_lk1, l2_diff_lq2, l2_diff_lk2, l2_diff_subln, l2_diff_w_o, l2_norm_mlp, l2_mlp_up, l2_mlp_down, l3_norm_mix, l3_fox_w_in, l3_fox_b_f, l3_fox_w_o, l3_norm_mlp, l3_mlp_up, l3_mlp_down, norm_final):
    hp, hs = x_prompt, x_sample
    state = []
    hp, hs, st = fox_mixer(hp, hs, l0_norm_mix, cache_l0_k, cache_l0_v, cache_l0_logf, page_table,
                           l0_fox_w_in, l0_fox_b_f, l0_fox_w_o)
    state += st
    hp, hs = mlp_both(hp, hs, l0_norm_mlp, l0_mlp_up, l0_mlp_down)
    hp, hs, st = nsa_mixer(hp, hs, l1_norm_mix, cache_l1_cmp_k, cache_l1_cmp_v, cache_l1_sel_k, cache_l1_sel_v,
                           state_l1_win_k, state_l1_win_v, page_table,
                           l1_nsa_w_in, l1_nsa_b_gate, l1_nsa_pe_k, l1_nsa_w1_k, l1_nsa_w2_k,
                           l1_nsa_pe_v, l1_nsa_w1_v, l1_nsa_w2_v, l1_nsa_w_o)
    state += st
    hp, hs = mlp_both(hp, hs, l1_norm_mlp, l1_mlp_up, l1_mlp_down)
    hp, hs, st = diff_mixer(hp, hs, l2_norm_mix, cache_l2_k, cache_l2_v, page_table, l2_diff_w_in,
                            l2_diff_lq1, l2_diff_lk1, l2_diff_lq2, l2_diff_lk2,
                            l2_diff_subln, l2_diff_w_o, DIFF_LAYER)
    state += st
    hp, hs = mlp_both(hp, hs, l2_norm_mlp, l2_mlp_up, l2_mlp_down)
    hp, hs, st = fox_mixer(hp, hs, l3_norm_mix, cache_l3_k, cache_l3_v, cache_l3_logf, page_table,
                           l3_fox_w_in, l3_fox_b_f, l3_fox_w_o)
    state += st
    hp, hs = mlp_both(hp, hs, l3_norm_mlp, l3_mlp_up, l3_mlp_down)
    y_p = final_norm(hp.reshape(-1, D_MODEL), norm_final).reshape(hp.shape)
    y_s = final_norm(hs.reshape(-1, D_MODEL), norm_final).reshape(hs.shape)
    return (y_p, y_s) + tuple(state)
```
